```python
import math
import jax, jax.numpy as jnp
from jax import lax
import numpy as np

D_MODEL = 4096
BATCH = 4
SEQ = 2048
DEPTH = 4
DEC_BATCH = 8
DEC_SEQ = 4
PAST_LEN = 8192
PAGE_SIZE = 128

POOL_W = D_MODEL // 4
POOL_WINDOWS = (2, 4, 8, 16)
POOL_GROUP = POOL_W // len(POOL_WINDOWS)
POOL_STATE = max(POOL_WINDOWS) - 1
CONV_W = D_MODEL // 4
CONV_K = 31
CONV_STATE = CONV_K - 1
ATT_W = D_MODEL - POOL_W - CONV_W
HEAD_DIM = 128
ATT_HEADS = ATT_W // HEAD_DIM
Q_BLOCK = 128
SB_SCALE = HEAD_DIM ** -0.5
QK_INIT = HEAD_DIM ** -0.25
SB_BIAS_NEAR = -4.0
SB_BIAS_FAR = -10.0
IN_COLS = POOL_W + 2 * CONV_W + 3 * ATT_W
PEER_HEADS = 8
PEER_QDIM = 256
PEER_HALF = PEER_QDIM // 2
N_KEYS = 128
N_EXPERTS = N_KEYS * N_KEYS
PEER_TOPK = 16
PEER_TOKEN_BLOCK = 64
N_MOD = 6
EPS = 1e-6

kernel_name = 'hybrid_pool_conv_stickbreak_peer_step'


def rmsnorm(x, g):
    xf = x.astype(jnp.float32)
    y = xf * lax.rsqrt(jnp.mean(xf * xf, axis=-1, keepdims=True) + EPS)
    return (y * g.astype(jnp.float32)).astype(x.dtype)


def layernorm(x, g, b):
    xf = x.astype(jnp.float32)
    mu = jnp.mean(xf, axis=-1, keepdims=True)
    var = jnp.mean(jnp.square(xf - mu), axis=-1, keepdims=True)
    y = (xf - mu) * lax.rsqrt(var + EPS)
    return (y * g.astype(jnp.float32) + b.astype(jnp.float32)).astype(x.dtype)


def pool_mix(a, prev, q_pos, w, scale):
    B, T, _ = a.shape
    full = jnp.concatenate([prev, a], axis=1)
    csum = jnp.cumsum(full.astype(jnp.float32), axis=1)
    csum = jnp.concatenate([jnp.zeros((B, 1, POOL_W), jnp.float32), csum], axis=1)
    outs = []
    for gi, win in enumerate(POOL_WINDOWS):
        lo, hi = gi * POOL_GROUP, (gi + 1) * POOL_GROUP
        end = csum[:, POOL_STATE + 1:POOL_STATE + 1 + T, lo:hi]
        start = csum[:, POOL_STATE + 1 - win:POOL_STATE + 1 - win + T, lo:hi]
        cnt = jnp.minimum(win, q_pos + 1).astype(jnp.float32)[None, :, None]
        d = ((end - start) / cnt - a[..., lo:hi].astype(jnp.float32)).astype(a.dtype)
        outs.append(d @ w[gi])
    y = jnp.concatenate(outs, axis=-1) * scale
    return y, full[:, -POOL_STATE:]


def conv_mix(u, prev, dw, dw_b, ln_g, ln_b, pw):
    val, gate = jnp.split(u, 2, axis=-1)
    z = val * jax.nn.sigmoid(gate)
    full = jnp.concatenate([prev, z], axis=1)
    y = lax.conv_general_dilated(full, dw[:, None, :], window_strides=(1,), padding='VALID',
                                 dimension_numbers=('NWC', 'WIO', 'NWC'),
                                 feature_group_count=CONV_W) + dw_b
    y = jax.nn.silu(layernorm(y, ln_g, ln_b)) @ pw
    return y, full[:, -CONV_STATE:]


def sb_block(qb, qpos, k, v, kpos, bias):
    z = jnp.einsum('bqhd,bkhd->bhqk', qb, k, preferred_element_type=jnp.float32) * SB_SCALE
    z = z + bias.astype(jnp.float32)[None, :, None, None]
    valid = kpos[None, :] < qpos[:, None]
    log1m = jnp.where(valid, jax.nn.log_sigmoid(-z), 0.0)
    suffix = lax.cumsum(log1m, axis=3, reverse=True) - log1m
    a = jnp.where(valid, jnp.exp(jax.nn.log_sigmoid(z) + suffix), 0.0)
    return jnp.einsum('bhqk,bkhd->bqhd', a.astype(v.dtype), v)


def stick_breaking(q, k, v, q_pos, bias):
    B, T, H, Dh = q.shape
    k_pos = jnp.arange(k.shape[1])
    qb = min(Q_BLOCK, T)
    nb = -(-T // qb)
    pad = nb * qb - T
    qp = jnp.pad(q, ((0, 0), (0, pad), (0, 0), (0, 0)))
    pp = jnp.pad(q_pos, (0, pad), mode='edge').reshape(nb, qb)
    qs = jnp.moveaxis(qp.reshape(B, nb, qb, H, Dh), 1, 0)
    out = lax.map(lambda args: sb_block(args[0], args[1], k, v, k_pos, bias), (qs, pp))
    out = jnp.moveaxis(out, 0, 1).reshape(B, nb * qb, H, Dh)
    return out[:, :T]


def peer(x, wq, subkeys, U, V):
    B, T, D = x.shape
    n = B * T
    xt = x.reshape(n, D)
    q = (xt @ wq).reshape(n, PEER_HEADS, 2, PEER_HALF)
    s = jnp.einsum('nhpd,pkd->nhpk', q, subkeys, preferred_element_type=jnp.float32)
    sv, si = lax.top_k(s, PEER_TOPK)
    cand = (sv[:, :, 0, :, None] + sv[:, :, 1, None, :]).reshape(n, PEER_HEADS, PEER_TOPK * PEER_TOPK)
    cidx = (si[:, :, 0, :, None] * N_KEYS + si[:, :, 1, None, :]).reshape(n, PEER_HEADS, PEER_TOPK * PEER_TOPK)
    tv, tpos = lax.top_k(cand, PEER_TOPK)
    eidx = jnp.take_along_axis(cidx, tpos, axis=-1)
    g = jax.nn.softmax(tv, axis=-1)
    blk = min(PEER_TOKEN_BLOCK, n)
    nb = -(-n // blk)
    pad = nb * blk - n
    xp = jnp.pad(xt, ((0, pad), (0, 0))).reshape(nb, blk, D)
    ep = jnp.pad(eidx, ((0, pad), (0, 0), (0, 0))).reshape(nb, blk, PEER_HEADS, PEER_TOPK)
    gp = jnp.pad(g, ((0, pad), (0, 0), (0, 0))).reshape(nb, blk, PEER_HEADS, PEER_TOPK)

    def block_fn(args):
        xb, eb, gb = args
        ub = jnp.take(U, eb, axis=0)
        h = jnp.einsum('nd,nhkd->nhk', xb, ub, preferred_element_type=jnp.float32)
        wgt = (gb * jax.nn.gelu(h)).astype(xb.dtype)
        vb = jnp.take(V, eb, axis=0)
        return jnp.einsum('nhk,nhkd->nd', wgt, vb)

    out = lax.map(block_fn, (xp, ep, gp)).reshape(nb * blk, D)[:n]
    return out.reshape(B, T, D)


def gather_pages(cache, page_table, l):
    pages = cache[page_table, l]
    DB, NP = page_table.shape
    return pages.reshape(DB, NP * PAGE_SIZE, ATT_HEADS, HEAD_DIM)


def run_trunk(x, c, q_pos, pool_prev, conv_prev, cache_k, cache_v, page_table, p):
    B, T, _ = x.shape
    c_act = jax.nn.silu(c)
    splits = [POOL_W, POOL_W + 2 * CONV_W, POOL_W + 2 * CONV_W + ATT_W, POOL_W + 2 * CONV_W + 2 * ATT_W]
    new_k, new_v, new_pool, new_conv = [], [], [], []
    for l in range(DEPTH):
        mod = (c_act @ p['w_mod'][l] + p['b_mod'][l])[:, None, :]
        sh1, sc1, g1, sh2, sc2, g2 = jnp.split(mod, N_MOD, axis=-1)
        h = rmsnorm(x, p['norm1_g'][l]) * (1 + sc1) + sh1
        u = h @ p['w_in'][l]
        a_in, b_in, q, k, v = jnp.split(u, splits, axis=-1)
        q = q.reshape(B, T, ATT_HEADS, HEAD_DIM)
        k = k.reshape(B, T, ATT_HEADS, HEAD_DIM)
        v = v.reshape(B, T, ATT_HEADS, HEAD_DIM)
        ya, pool_state = pool_mix(a_in, pool_prev[l], q_pos, p['pool_w'][l], p['pool_scale'][l])
        yb, conv_state = conv_mix(b_in, conv_prev[l], p['conv_dw'][l], p['conv_dw_b'][l],
                                  p['conv_ln_g'][l], p['conv_ln_b'][l], p['conv_pw'][l])
        if cache_k is None:
            k_all, v_all = k, v
        else:
            k_all = jnp.concatenate([gather_pages(cache_k, page_table, l), k], axis=1)
            v_all = jnp.concatenate([gather_pages(cache_v, page_table, l), v], axis=1)
        yc = stick_breaking(q, k_all, v_all, q_pos, p['sb_bias'][l]).reshape(B, T, ATT_W)
        gmix = p['mix_norm_g'][l]
        merged = jnp.concatenate([rmsnorm(ya, gmix[:POOL_W]),
                                  rmsnorm(yb, gmix[POOL_W:POOL_W + CONV_W]),
                                  rmsnorm(yc, gmix[POOL_W + CONV_W:])], axis=-1)
        x = x + g1 * (merged @ p['w_out'][l])
        h2 = rmsnorm(x, p['norm2_g'][l]) * (1 + sc2) + sh2
        x = x + g2 * peer(h2, p['peer_wq'][l], p['peer_subkeys'][l], p['peer_u'][l], p['peer_v'][l])
        new_k.append(k)
        new_v.append(v)
        new_pool.append(pool_state)
        new_conv.append(conv_state)
    y = rmsnorm(x, p['final_g'])
    return y, jnp.stack(new_k, 1), jnp.stack(new_v, 1), jnp.stack(new_pool, 0), jnp.stack(new_conv, 0)


def setup_inputs(seed: int = 0) -> dict:
    key = jax.random.key(seed)
    ks = jax.random.split(key, 40)
    f32 = jnp.float32
    n_pages = PAST_LEN // PAGE_SIZE
    n_used = DEC_BATCH * n_pages
    n_phys = n_used + max(1, n_used // 4)

    def nrm(k, shape, s):
        return s * jax.random.normal(k, shape, f32)

    def gain(k, shape):
        return 1.0 + 0.01 * jax.random.normal(k, shape, f32)

    page_table = jax.random.permutation(ks[6], n_phys)[:n_used].reshape(DEC_BATCH, n_pages).astype(jnp.int32)
    col_scale = jnp.concatenate([jnp.ones((POOL_W + 2 * CONV_W,), f32),
                                 jnp.full((2 * ATT_W,), QK_INIT, f32),
                                 jnp.ones((ATT_W,), f32)])
    w_in = nrm(ks[12], (DEPTH, D_MODEL, IN_COLS), D_MODEL ** -0.5) * col_scale
    head_frac = jnp.arange(ATT_HEADS, dtype=f32) / max(ATT_HEADS - 1, 1)
    sb_bias = (SB_BIAS_NEAR + (SB_BIAS_FAR - SB_BIAS_NEAR) * head_frac)[None, :] \
        + 0.1 * jax.random.normal(ks[28], (DEPTH, ATT_HEADS), f32)
    return {
        'x_prompt': nrm(ks[0], (BATCH, SEQ, D_MODEL), 1.0),
        'x_sample': nrm(ks[1], (DEC_BATCH, DEC_SEQ, D_MODEL), 1.0),
        'c_prompt': nrm(ks[2], (BATCH, D_MODEL), 1.0),
        'c_sample': nrm(ks[3], (DEC_BATCH, D_MODEL), 1.0),
        'cache_k': nrm(ks[4], (n_phys, DEPTH, PAGE_SIZE, ATT_HEADS, HEAD_DIM), QK_INIT),
        'cache_v': nrm(ks[5], (n_phys, DEPTH, PAGE_SIZE, ATT_HEADS, HEAD_DIM), 1.0),
        'page_table': page_table,
        'state_pool': nrm(ks[7], (DEPTH, DEC_BATCH, POOL_STATE, POOL_W), 1.0),
        'state_conv': nrm(ks[8], (DEPTH, DEC_BATCH, CONV_STATE, CONV_W), 0.5),
        'norm1_g': gain(ks[9], (DEPTH, D_MODEL)),
        'w_mod': nrm(ks[10], (DEPTH, D_MODEL, N_MOD * D_MODEL), 0.5 * D_MODEL ** -0.5),
        'b_mod': nrm(ks[11], (DEPTH, N_MOD * D_MODEL), 0.01),
        'w_in': w_in,
        'sb_bias': sb_bias,
        'pool_w': nrm(ks[13], (DEPTH, len(POOL_WINDOWS), POOL_GROUP, POOL_GROUP), POOL_GROUP ** -0.5),
        'pool_scale': gain(ks[14], (DEPTH, POOL_W)),
        'conv_dw': nrm(ks[15], (DEPTH, CONV_K, CONV_W), CONV_K ** -0.5),
        'conv_dw_b': nrm(ks[16], (DEPTH, CONV_W), 0.01),
        'conv_ln_g': gain(ks[17], (DEPTH, CONV_W)),
        'conv_ln_b': nrm(ks[18], (DEPTH, CONV_W), 0.01),
        'conv_pw': nrm(ks[19], (DEPTH, CONV_W, CONV_W), CONV_W ** -0.5),
        'mix_norm_g': gain(ks[20], (DEPTH, D_MODEL)),
        'w_out': nrm(ks[21], (DEPTH, D_MODEL, D_MODEL), D_MODEL ** -0.5),
        'norm2_g': gain(ks[22], (DEPTH, D_MODEL)),
        'peer_wq': nrm(ks[23], (DEPTH, D_MODEL, PEER_HEADS * PEER_QDIM), D_MODEL ** -0.5),
        'peer_subkeys': nrm(ks[24], (DEPTH, 2, N_KEYS, PEER_HALF), PEER_HALF ** -0.5),
        'peer_u': nrm(ks[25], (DEPTH, N_EXPERTS, D_MODEL), D_MODEL ** -0.5),
        'peer_v': nrm(ks[26], (DEPTH, N_EXPERTS, D_MODEL), 0.5),
        'final_g': gain(ks[27], (D_MODEL,)),
    }


def reference(x_prompt, x_sample, c_prompt, c_sample, cache_k, cache_v, page_table, state_pool, state_conv,
              norm1_g, w_mod, b_mod, w_in, sb_bias, pool_w, pool_scale, conv_dw, conv_dw_b, conv_ln_g, conv_ln_b,
              conv_pw, mix_norm_g, w_out, norm2_g, peer_wq, peer_subkeys, peer_u, peer_v, final_g):
    p = dict(norm1_g=norm1_g, w_mod=w_mod, b_mod=b_mod, w_in=w_in, sb_bias=sb_bias, pool_w=pool_w,
             pool_scale=pool_scale, conv_dw=conv_dw, conv_dw_b=conv_dw_b, conv_ln_g=conv_ln_g,
             conv_ln_b=conv_ln_b, conv_pw=conv_pw, mix_norm_g=mix_norm_g, w_out=w_out, norm2_g=norm2_g,
             peer_wq=peer_wq, peer_subkeys=peer_subkeys, peer_u=peer_u, peer_v=peer_v, final_g=final_g)
    Bp, Tp, _ = x_prompt.shape
    _, Ts, _ = x_sample.shape
    past_len = page_table.shape[1] * PAGE_SIZE
    pool0 = jnp.zeros((DEPTH, Bp, POOL_STATE, POOL_W), x_prompt.dtype)
    conv0 = jnp.zeros((DEPTH, Bp, CONV_STATE, CONV_W), x_prompt.dtype)
    y_prompt, k_prompt, v_prompt, pool_prompt, conv_prompt = run_trunk(
        x_prompt, c_prompt, jnp.arange(Tp), pool0, conv0, None, None, None, p)
    y_sample, k_sample, v_sample, pool_sample, conv_sample = run_trunk(
        x_sample, c_sample, past_len + jnp.arange(Ts), state_pool, state_conv, cache_k, cache_v, page_table, p)
    return (y_prompt, y_sample, k_prompt, v_prompt, k_sample, v_sample, pool_prompt, pool_sample, conv_prompt, conv_sample)
```

```python
import functools

import jax
import jax.numpy as jnp
from jax import lax
from jax.experimental import pallas as pl
from jax.experimental.pallas import tpu as pltpu

F32 = jnp.float32
BF16 = jnp.bfloat16

SUBLANES = 8
LANES = 128
VMEM_LIMIT_MB = 56

DEPTH = 4
PAGE_SIZE = 128
POOL_WINDOWS = (2, 4, 8, 16)
POOL_STATE = 15
POOL_HALO = 16
CONV_K = 31
CONV_STATE = CONV_K - 1
CONV_HALO = 32
HEAD_DIM = 128
SB_SCALE = HEAD_DIM ** -0.5
PEER_HEADS = 8
PEER_HALF = 128
N_KEYS = 128
PEER_TOPK = 16
N_MOD = 6
EPS = 1e-6
DEC_Q_PAD = 8

_NT = (((1,), (1,)), ((), ()))


def _cparams(semantics):
    return pltpu.CompilerParams(dimension_semantics=semantics, vmem_limit_bytes=VMEM_LIMIT_MB << 20)


def _rms(x, g):
    return x * lax.rsqrt(jnp.mean(x * x, axis=-1, keepdims=True) + EPS) * g


def _mod_kernel(c_ref, w_ref, b_ref, o_ref):
    c = c_ref[...]
    a = (c * jax.nn.sigmoid(c)).astype(BF16)
    o_ref[...] = jnp.dot(a, w_ref[...].astype(BF16), preferred_element_type=F32) + b_ref[...]


def _modulation(c_all, w_mod, b_mod, tn=512):
    rows, d = c_all.shape
    depth, _, cols = w_mod.shape
    return pl.pallas_call(
        _mod_kernel,
        grid=(depth, cols // tn),
        in_specs=[
            pl.BlockSpec((rows, d), lambda l, j: (0, 0)),
            pl.BlockSpec((None, d, tn), lambda l, j: (l, 0, j)),
            pl.BlockSpec((None, 1, tn), lambda l, j: (l, 0, j)),
        ],
        out_specs=pl.BlockSpec((None, rows, tn), lambda l, j: (l, 0, j)),
        out_shape=jax.ShapeDtypeStruct((depth, rows, cols), F32),
        compiler_params=_cparams(("arbitrary", "arbitrary")),
        name="modulation",
    )(c_all, w_mod, b_mod.reshape(depth, 1, cols))


def _normproj_kernel(x_ref, g_ref, sc_ref, sh_ref, w_ref, o_ref, h_ref):
    @pl.when(pl.program_id(1) == 0)
    def _():
        h = _rms(x_ref[...], g_ref[...]) * (1.0 + sc_ref[...]) + sh_ref[...]
        h_ref[...] = h.astype(BF16)

    o_ref[...] = jnp.dot(h_ref[...], w_ref[...], preferred_element_type=F32)


def _normproj(x, g, mod3, k_scale, k_shift, w, tm, tn, emit_h):
    n, d = x.shape
    cols = w.shape[1]
    groups, r, _ = mod3.shape
    tiles_per_group = n // tm // groups
    mod_spec = lambda k: pl.BlockSpec((None, r, d), lambda i, j: (i // tiles_per_group, 0, k))
    in_specs = [
        pl.BlockSpec((tm, d), lambda i, j: (i, 0)),
        pl.BlockSpec((1, d), lambda i, j: (0, 0)),
        mod_spec(k_scale),
        mod_spec(k_shift),
        pl.BlockSpec((d, tn), lambda i, j: (0, j)),
    ]
    o_spec = pl.BlockSpec((tm, tn), lambda i, j: (i, j))
    o_shape = jax.ShapeDtypeStruct((n, cols), F32)
    if emit_h:
        out_specs = [o_spec, pl.BlockSpec((tm, d), lambda i, j: (i, 0))]
        out_shape = [o_shape, jax.ShapeDtypeStruct((n, d), BF16)]
        scratch = []
    else:
        out_specs, out_shape, scratch = o_spec, o_shape, [pltpu.VMEM((tm, d), BF16)]
    return pl.pallas_call(
        _normproj_kernel,
        grid=(n // tm, cols // tn),
        in_specs=in_specs,
        out_specs=out_specs,
        out_shape=out_shape,
        scratch_shapes=scratch,
        compiler_params=_cparams(("arbitrary", "arbitrary")),
        name="normproj_h" if emit_h else "normproj",
    )(x, g.reshape(1, d), mod3, mod3, w)


def _tri_ones(tk):
    r = lax.broadcasted_iota(jnp.int32, (tk, tk + LANES), 0)
    c = lax.broadcasted_iota(jnp.int32, (tk, tk + LANES), 1)
    return jnp.where(c >= tk, 1.0, jnp.where(r > c, 1.0, 0.0)).astype(BF16)


def _sb_block(z, run, tri, valid):
    tk = z.shape[1]
    sp = jnp.maximum(z, 0.0) + jnp.log(1.0 + jnp.exp(-jnp.abs(z)))
    l1m = -sp if valid is None else jnp.where(valid, -sp, 0.0)
    sr = jnp.dot(l1m.astype(BF16), tri, preferred_element_type=F32)
    run_b = run if tk == LANES else jnp.concatenate([run] * (tk // LANES), axis=-1)
    a = jnp.exp((z - sp) + sr[:, :tk] + run_b)
    if valid is not None:
        a = jnp.where(valid, a, 0.0)
    return a, run + sr[:, tk:]


def _attn_kernel(bias_ref, q_ref, k_ref, v_ref, o_ref, q_scr, k_scr, v_scr, *, blk):
    t = q_ref.shape[0]
    bias = bias_ref[pl.program_id(1)]
    q_scr[...] = (q_ref[...] * SB_SCALE).astype(BF16)
    k_scr[...] = k_ref[...].astype(BF16)
    v_scr[...] = v_ref[...].astype(BF16)
    tri = _tri_ones(blk)
    row = lax.broadcasted_iota(jnp.int32, (blk, blk), 0)
    col = lax.broadcasted_iota(jnp.int32, (blk, blk), 1)
    diag_valid = col < row

    def q_block(qi, _):
        q0 = pl.multiple_of(qi * blk, blk)
        q = q_scr[pl.ds(q0, blk), :]

        def k_block(k0, carry, valid):
            acc, run = carry
            z = lax.dot_general(q, k_scr[pl.ds(k0, blk), :], _NT, preferred_element_type=F32) + bias
            a, run = _sb_block(z, run, tri, valid)
            acc = acc + jnp.dot(a.astype(BF16), v_scr[pl.ds(k0, blk), :], preferred_element_type=F32)
            return acc, run

        carry = (jnp.zeros((blk, HEAD_DIM), F32), jnp.zeros((blk, LANES), F32))
        carry = k_block(q0, carry, diag_valid)

        def body(s, carry):
            k0 = pl.multiple_of((qi - 1 - s) * blk, blk)
            return k_block(k0, carry, None)

        acc, _ = lax.fori_loop(0, qi, body, carry)
        o_ref[pl.ds(q0, blk), :] = acc
        return 0

    lax.fori_loop(0, t // blk, q_block, 0)


def _attention(u3, sb_bias, n_heads, q_col, k_col, v_col, blk=256):
    b, t, _ = u3.shape
    spec = lambda c0: pl.BlockSpec((None, t, HEAD_DIM), lambda i, h: (i, 0, c0 + h))
    return pl.pallas_call(
        functools.partial(_attn_kernel, blk=blk),
        grid=(b, n_heads),
        in_specs=[pl.BlockSpec(memory_space=pltpu.SMEM), spec(q_col), spec(k_col), spec(v_col)],
        out_specs=pl.BlockSpec((None, t, HEAD_DIM), lambda i, h: (i, 0, h)),
        out_shape=jax.ShapeDtypeStruct((b, t, n_heads * HEAD_DIM), F32),
        scratch_shapes=[pltpu.VMEM((t, HEAD_DIM), BF16)] * 3,
        compiler_params=_cparams(("arbitrary", "arbitrary")),
        name="attn",
    )(sb_bias, u3, u3, u3)


def _dec_attn_kernel(pt_ref, q_ref, bias_ref, kn_ref, vn_ref, kc_ref, vc_ref, o_ref, qbd_scr, acc_scr, run_scr,
                     *, n_heads):
    j = pl.program_id(1)
    rows = n_heads * DEC_Q_PAD
    width = n_heads * HEAD_DIM

    @pl.when(j == 0)
    def _():
        q = jnp.concatenate([q_ref[...] * SB_SCALE] * n_heads, axis=0)
        r = jnp.right_shift(lax.broadcasted_iota(jnp.int32, (rows, width), 0), DEC_Q_PAD.bit_length() - 1)
        c = jnp.right_shift(lax.broadcasted_iota(jnp.int32, (rows, width), 1), HEAD_DIM.bit_length() - 1)
        qbd_scr[...] = jnp.where(r == c, q, 0.0).astype(BF16)
        acc_scr[...] = jnp.zeros_like(acc_scr)
        run_scr[...] = jnp.zeros_like(run_scr)

    tri = _tri_ones(PAGE_SIZE)

    def page(k_page, v_page, valid):
        z = lax.dot_general(qbd_scr[...], k_page.astype(BF16), _NT, preferred_element_type=F32) + bias_ref[...]
        a, run = _sb_block(z, run_scr[...], tri, valid)
        run_scr[...] = run
        acc_scr[...] += jnp.dot(a.astype(BF16), v_page.astype(BF16), preferred_element_type=F32)

    @pl.when(j == 0)
    def _():
        qi = jnp.bitwise_and(lax.broadcasted_iota(jnp.int32, (rows, PAGE_SIZE), 0), DEC_Q_PAD - 1)
        kc = lax.broadcasted_iota(jnp.int32, (rows, PAGE_SIZE), 1)
        page(kn_ref[...], vn_ref[...], kc < qi)

    @pl.when(j > 0)
    def _():
        page(kc_ref[...], vc_ref[...], None)

    @pl.when(j == pl.num_programs(1) - 1)
    def _():
        for h in range(n_heads):
            o_ref[:, h * HEAD_DIM:(h + 1) * HEAD_DIM] = acc_scr[h * DEC_Q_PAD:(h + 1) * DEC_Q_PAD,
                                                                h * HEAD_DIM:(h + 1) * HEAD_DIM]


def _dec_attention(q, k_new, v_new, cache_k, cache_v, page_table, layer, sb_bias):
    db, ts, width = q.shape
    n_heads = width // HEAD_DIM
    n_pages = page_table.shape[1]
    rows = n_heads * DEC_Q_PAD
    q_pad = jnp.pad(q, ((0, 0), (0, DEC_Q_PAD - ts), (0, 0)))
    kn = jnp.pad(k_new, ((0, 0), (0, PAGE_SIZE - ts), (0, 0)))
    vn = jnp.pad(v_new, ((0, 0), (0, PAGE_SIZE - ts), (0, 0)))
    bias = jnp.broadcast_to(jnp.repeat(sb_bias, DEC_Q_PAD)[:, None], (rows, PAGE_SIZE))
    page_of = lambda b, j, pt: pt[b, jnp.minimum(n_pages - j, n_pages - 1)]
    per_seq = lambda r: pl.BlockSpec((None, r, width), lambda b, j, pt: (b, 0, 0))
    cache_spec = pl.BlockSpec((None, None, PAGE_SIZE, width), lambda b, j, pt: (page_of(b, j, pt), layer, 0, 0))
    out = pl.pallas_call(
        functools.partial(_dec_attn_kernel, n_heads=n_heads),
        grid_spec=pltpu.PrefetchScalarGridSpec(
            num_scalar_prefetch=1,
            grid=(db, n_pages + 1),
            in_specs=[per_seq(DEC_Q_PAD), pl.BlockSpec((rows, PAGE_SIZE), lambda b, j, pt: (0, 0)),
                      per_seq(PAGE_SIZE), per_seq(PAGE_SIZE), cache_spec, cache_spec],
            out_specs=per_seq(DEC_Q_PAD),
            scratch_shapes=[pltpu.VMEM((rows, width), BF16), pltpu.VMEM((rows, width), F32),
                            pltpu.VMEM((rows, LANES), F32)],
        ),
        out_shape=jax.ShapeDtypeStruct((db, DEC_Q_PAD, width), F32),
        compiler_params=_cparams(("arbitrary", "arbitrary")),
        name="dec_attn",
    )(page_table, q_pad, bias, kn, vn, cache_k, cache_v)
    return out[:, :ts]


def _mix_kernel(a_ref, ah_ref, val_ref, gate_ref, zh0_ref, zh1_ref, pw_ref, ps_ref, dw_ref, dwb_ref, lng_ref,
                lnb_ref, cpw_ref, gm_ref, ya_ref, yb_ref, pst_ref, cst_ref, fa_scr, fz_scr, y_scr,
                *, t_real, pos0, halo_is_state):
    t = pl.program_id(1)
    tt, pool_w = a_ref.shape
    conv_w = val_ref.shape[1]
    group_w = pool_w // len(POOL_WINDOWS)

    a = a_ref[...]
    ah = ah_ref[...]
    if not halo_is_state:
        ah = jnp.where(t == 0, 0.0, ah)
    fa_scr[0:POOL_HALO, :] = ah
    fa_scr[POOL_HALO:, :] = a

    @pl.when(t == pl.num_programs(1) - 1)
    def _():
        end = POOL_HALO + t_real
        pst_ref[...] = fa_scr[end - POOL_STATE:end, :]

    pos = pos0 + t * tt + lax.broadcasted_iota(jnp.int32, (tt, group_w), 0)
    outs = []
    for gi, win in enumerate(POOL_WINDOWS):
        cs = slice(gi * group_w, (gi + 1) * group_w)
        acc = fa_scr[POOL_HALO:POOL_HALO + tt, cs]
        for w in range(1, win):
            acc = acc + fa_scr[POOL_HALO - w:POOL_HALO - w + tt, cs]
        cnt = jnp.minimum(win, pos + 1).astype(F32)
        d = acc / cnt - a[:, cs]
        outs.append(jnp.dot(d.astype(BF16), pw_ref[gi], preferred_element_type=F32))
    ya = jnp.concatenate(outs, axis=-1) * ps_ref[...]
    ya_ref[...] = _rms(ya, gm_ref[:, 0:pool_w])

    if halo_is_state:
        zh = zh0_ref[...]
    else:
        zh = jnp.where(t == 0, 0.0, zh0_ref[...] * jax.nn.sigmoid(zh1_ref[...]))
    fz_scr[0:CONV_HALO, :] = zh
    fz_scr[CONV_HALO:, :] = val_ref[...] * jax.nn.sigmoid(gate_ref[...])

    @pl.when(t == pl.num_programs(1) - 1)
    def _():
        end = CONV_HALO + t_real
        cst_ref[...] = fz_scr[end - CONV_STATE:end, :]

    base = CONV_HALO - CONV_STATE
    for c in range(conv_w // LANES):
        cs = slice(c * LANES, (c + 1) * LANES)
        acc = jnp.zeros((tt, LANES), F32) + dwb_ref[:, cs]
        for lo in range(SUBLANES):
            n_hi = len(range(lo, CONV_K, SUBLANES))
            win = fz_scr[base + lo:base + lo + tt + SUBLANES * (n_hi - 1), cs]
            for hi in range(n_hi):
                j = SUBLANES * hi + lo
                acc = acc + dw_ref[j:j + 1, cs] * win[SUBLANES * hi:SUBLANES * hi + tt]
        y_scr[:, cs] = acc
    y = y_scr[...]
    mu = jnp.mean(y, axis=-1, keepdims=True)
    yc = y - mu
    var = jnp.mean(yc * yc, axis=-1, keepdims=True)
    yn = yc * lax.rsqrt(var + EPS) * lng_ref[...] + lnb_ref[...]
    s = yn * jax.nn.sigmoid(yn)
    yb = jnp.dot(s.astype(BF16), cpw_ref[...], preferred_element_type=F32)
    yb_ref[...] = _rms(yb, gm_ref[:, pool_w:pool_w + conv_w])


def _mix(u3, a_halo, z_halo, weights, gmix, *, tt, t_real, pos0, halo_is_state):
    b, t, _ = u3.shape
    pool_w16, pool_scale, conv_dw, conv_dw_b, ln_g, ln_b, conv_pw16 = weights
    w = pool_scale.shape[-1]
    nt = t // tt
    tile = lambda c: pl.BlockSpec((None, tt, w), lambda i, s: (i, s, c))
    if halo_is_state:
        ah_spec = pl.BlockSpec((None, POOL_HALO, w), lambda i, s: (i, 0, 0))
        zh_specs = [pl.BlockSpec((None, CONV_HALO, w), lambda i, s: (i, 0, 0))] * 2
        halos = (a_halo, z_halo, z_halo)
    else:
        prev = lambda rows, c: pl.BlockSpec(
            (None, rows, w), lambda i, s: (i, jnp.maximum(s * (tt // rows) - 1, 0), c))
        ah_spec = prev(POOL_HALO, 0)
        zh_specs = [prev(CONV_HALO, 1), prev(CONV_HALO, 2)]
        halos = (u3, u3, u3)
    full = lambda shape: pl.BlockSpec(shape, lambda i, s: (0,) * len(shape))
    row = lambda x: x.reshape(1, -1)
    return pl.pallas_call(
        functools.partial(_mix_kernel, t_real=t_real, pos0=pos0, halo_is_state=halo_is_state),
        grid=(b, nt),
        in_specs=[tile(0), ah_spec, tile(1), tile(2), *zh_specs,
                  full(pool_w16.shape), full((1, w)), full(conv_dw.shape), full((1, w)), full((1, w)),
                  full((1, w)), full(conv_pw16.shape), full((1, gmix.shape[-1]))],
        out_specs=[pl.BlockSpec((None, tt, w), lambda i, s: (i, s, 0))] * 2
        + [pl.BlockSpec((None, POOL_STATE, w), lambda i, s: (i, 0, 0)),
           pl.BlockSpec((None, CONV_STATE, w), lambda i, s: (i, 0, 0))],
        out_shape=[jax.ShapeDtypeStruct((b, t, w), F32)] * 2
        + [jax.ShapeDtypeStruct((b, POOL_STATE, w), F32), jax.ShapeDtypeStruct((b, CONV_STATE, w), F32)],
        scratch_shapes=[pltpu.VMEM((POOL_HALO + tt, w), F32), pltpu.VMEM((CONV_HALO + tt, w), F32),
                        pltpu.VMEM((tt, w), F32)],
        compiler_params=_cparams(("arbitrary", "arbitrary")),
        name="mix",
    )(u3, halos[0], u3, u3, halos[1], halos[2], pool_w16, row(pool_scale), conv_dw, row(conv_dw_b), row(ln_g),
      row(ln_b), conv_pw16, row(gmix))


def _outproj_kernel(ya_ref, yb_ref, yc_ref, gm_ref, w_ref, x_ref, g1_ref, o_ref, m_scr):
    @pl.when(pl.program_id(1) == 0)
    def _():
        wa, wb = ya_ref.shape[1], yb_ref.shape[1]
        m_scr[:, 0:wa] = ya_ref[...].astype(BF16)
        m_scr[:, wa:wa + wb] = yb_ref[...].astype(BF16)
        m_scr[:, wa + wb:] = _rms(yc_ref[...], gm_ref[:, wa + wb:]).astype(BF16)

    o_ref[...] = x_ref[...] + g1_ref[...] * jnp.dot(m_scr[...], w_ref[...], preferred_element_type=F32)


def _outproj(ya, yb, yc, gmix, w, x, mod3, k_gate, tm, tn):
    n, d = x.shape
    groups, r, _ = mod3.shape
    tiles_per_group = n // tm // groups
    rows = lambda width: pl.BlockSpec((tm, width), lambda i, j: (i, 0))
    return pl.pallas_call(
        _outproj_kernel,
        grid=(n // tm, d // tn),
        in_specs=[rows(ya.shape[1]), rows(yb.shape[1]), rows(yc.shape[1]),
                  pl.BlockSpec((1, d), lambda i, j: (0, 0)),
                  pl.BlockSpec((d, tn), lambda i, j: (0, j)),
                  pl.BlockSpec((tm, tn), lambda i, j: (i, j)),
                  pl.BlockSpec((None, r, tn), lambda i, j: (i // tiles_per_group, 0, k_gate * (d // tn) + j))],
        out_specs=pl.BlockSpec((tm, tn), lambda i, j: (i, j)),
        out_shape=jax.ShapeDtypeStruct((n, d), F32),
        scratch_shapes=[pltpu.VMEM((tm, d), BF16)],
        compiler_params=_cparams(("arbitrary", "arbitrary")),
        name="outproj",
    )(ya, yb, yc, gmix.reshape(1, d), w, x, mod3)


def _sort_network(n):
    pairs = []

    def merge(lo, hi, r):
        step = r * 2
        if step < hi - lo:
            merge(lo, hi, step)
            merge(lo + r, hi, step)
            pairs.extend((i, i + r) for i in range(lo + r, hi - r, step))
        else:
            pairs.append((lo, lo + r))

    def sort(lo, hi):
        if hi - lo >= 1:
            mid = lo + (hi - lo) // 2
            sort(lo, mid)
            sort(mid + 1, hi)
            merge(lo, hi, 1)

    sort(0, n - 1)
    return pairs


_SORT16 = _sort_network(PEER_TOPK)
_BITONIC16 = [(i, i + d) for d in (8, 4, 2, 1) for i in range(PEER_TOPK) if not i & d]


def _exchange(v, pairs):
    for i, j in pairs:
        v[i], v[j] = jnp.maximum(v[i], v[j]), jnp.minimum(v[i], v[j])
    return v


def _merge_sublanes(v):
    for shift in (4, 2, 1):
        other = [pltpu.roll(x, shift, 0) for x in v]
        v = [jnp.maximum(v[i], other[PEER_TOPK - 1 - i]) for i in range(PEER_TOPK)]
        v = _exchange(v, _BITONIC16)
    return v


def _top16_keys(s):
    v = [s[SUBLANES * i:SUBLANES * (i + 1), :] for i in range(N_KEYS // SUBLANES)]
    return _merge_sublanes(_exchange(v, _SORT16))


def _route_kernel(q_ref, sk_ref, s0_ref, s1_ref, tau_ref, beta_ref):
    sub = lax.broadcasted_iota(jnp.int32, (SUBLANES, LANES), 0)
    sk = [sk_ref[p].astype(BF16) for p in range(2)]
    tau_all = jnp.zeros((SUBLANES, LANES), F32)
    beta_all = jnp.zeros((SUBLANES, LANES), F32)
    for h in range(PEER_HEADS):
        top = []
        for p, s_ref in enumerate((s0_ref, s1_ref)):
            c0 = (2 * h + p) * PEER_HALF
            qc = q_ref[:, c0:c0 + PEER_HALF].astype(BF16)
            s = lax.dot_general(sk[p], qc, _NT, preferred_element_type=F32)
            s_ref[h] = s
            top.append(_top16_keys(s))
        a, b = top
        b_lo, b_hi = b[SUBLANES - 1], b[2 * SUBLANES - 1]
        for r in range(SUBLANES - 2, -1, -1):
            b_lo = jnp.where(sub == r, b[r], b_lo)
            b_hi = jnp.where(sub == r, b[SUBLANES + r], b_hi)
        cand = _exchange([x + b_lo for x in a], _SORT16)
        extra = a[0] + b_hi
        for i in range(PEER_TOPK):
            cand[i], extra = jnp.maximum(cand[i], extra), jnp.minimum(cand[i], extra)
        tv = _merge_sublanes(cand)
        zsum = jnp.ones((SUBLANES, LANES), F32)
        for r in range(1, PEER_TOPK):
            zsum = zsum + jnp.exp(tv[r] - tv[0])
        tau_all = jnp.where(sub == h, tv[PEER_TOPK - 1], tau_all)
        beta_all = jnp.where(sub == h, -(tv[0] + jnp.log(zsum)), beta_all)
    tau_ref[...] = tau_all
    beta_ref[...] = beta_all


def _route(q, subkeys):
    n, qc = q.shape
    score = pl.BlockSpec((PEER_HEADS, N_KEYS, LANES), lambda i: (0, 0, i))
    head_row = pl.BlockSpec((PEER_HEADS, LANES), lambda i: (0, i))
    return pl.pallas_call(
        _route_kernel,
        grid=(n // LANES,),
        in_specs=[pl.BlockSpec((LANES, qc), lambda i: (i, 0)),
                  pl.BlockSpec(subkeys.shape, lambda i: (0, 0, 0))],
        out_specs=[score, score, head_row, head_row],
        out_shape=[jax.ShapeDtypeStruct((PEER_HEADS, N_KEYS, n), F32)] * 2
        + [jax.ShapeDtypeStruct((PEER_HEADS, n), F32)] * 2,
        compiler_params=_cparams(("arbitrary",)),
        name="route",
    )(q, subkeys)


def _gelu_tanh(x):
    return 0.5 * x * (1.0 + jnp.tanh(0.7978845608028654 * (x + 0.044715 * (x * x * x))))


def _peer_kernel(h_ref, u_ref, v_ref, s0_ref, s1_ref, tau_ref, beta_ref, x_ref, g2_ref, o_ref, w_scr):
    t = pl.program_id(1)
    te = u_ref.shape[0]
    tm = h_ref.shape[0]

    @pl.when(t == 0)
    def _():
        o_ref[...] = jnp.zeros_like(o_ref)

    ht = lax.dot_general(u_ref[...], h_ref[...], _NT, preferred_element_type=F32)
    for ii in range(te // N_KEYS):
        for cc in range(tm // LANES):
            ls = slice(cc * LANES, (cc + 1) * LANES)
            gate = jnp.zeros((N_KEYS, LANES), F32)
            for h in range(PEER_HEADS):
                p = s0_ref[ii, h:h + 1, ls] + s1_ref[h, :, ls]
                gate = gate + jnp.where(p >= tau_ref[h:h + 1, ls], jnp.exp(p + beta_ref[h:h + 1, ls]), 0.0)
            w_scr[ii * N_KEYS:(ii + 1) * N_KEYS, ls] = _gelu_tanh(ht[ii * N_KEYS:(ii + 1) * N_KEYS, ls]) * gate
    w = w_scr[...].T.astype(BF16)
    o_ref[...] += jnp.dot(w, v_ref[...], preferred_element_type=F32)

    @pl.when(t == pl.num_programs(1) - 1)
    def _():
        o_ref[...] = x_ref[...] + g2_ref[...] * o_ref[...]


def _peer(h, u16, v16, s0t, s1t, tau, beta, x, mod3, k_gate, tm, te):
    n, d = x.shape
    n_exp = u16.shape[0]
    groups, r, _ = mod3.shape
    tiles_per_group = n // tm // groups
    once = dict(pipeline_mode=pl.Buffered(1))
    return pl.pallas_call(
        _peer_kernel,
        grid=(n // tm, n_exp // te),
        in_specs=[pl.BlockSpec((tm, d), lambda i, t: (i, 0), **once),
                  pl.BlockSpec((te, d), lambda i, t: (t, 0)),
                  pl.BlockSpec((te, d), lambda i, t: (t, 0)),
                  pl.BlockSpec((te // N_KEYS, PEER_HEADS, tm), lambda i, t: (t, 0, i)),
                  pl.BlockSpec((PEER_HEADS, N_KEYS, tm), lambda i, t: (0, 0, i), **once),
                  pl.BlockSpec((PEER_HEADS, tm), lambda i, t: (0, i)),
                  pl.BlockSpec((PEER_HEADS, tm), lambda i, t: (0, i)),
                  pl.BlockSpec((tm, d), lambda i, t: (i, 0), **once),
                  pl.BlockSpec((None, r, d), lambda i, t: (i // tiles_per_group, 0, k_gate))],
        out_specs=pl.BlockSpec((tm, d), lambda i, t: (i, 0)),
        out_shape=jax.ShapeDtypeStruct((n, d), F32),
        scratch_shapes=[pltpu.VMEM((te, tm), F32)],
        compiler_params=_cparams(("arbitrary", "arbitrary")),
        name="peer",
    )(h, u16, v16, jnp.swapaxes(s0t, 0, 1), s1t, tau, beta, x, mod3)


def _final_norm_kernel(x_ref, g_ref, o_ref):
    o_ref[...] = _rms(x_ref[...], g_ref[...])


def _final_norm(x, g, tm):
    n, d = x.shape
    return pl.pallas_call(
        _final_norm_kernel,
        grid=(n // tm,),
        in_specs=[pl.BlockSpec((tm, d), lambda i: (i, 0)), pl.BlockSpec((1, d), lambda i: (0, 0))],
        out_specs=pl.BlockSpec((tm, d), lambda i: (i, 0)),
        out_shape=jax.ShapeDtypeStruct((n, d), F32),
        compiler_params=_cparams(("arbitrary",)),
        name="final_norm",
    )(x, g.reshape(1, d))


def _trunk(x3, mod_rows, weights, final_g, *, tm, tn, tt, t_real_pad, pos0, peer_tm, peer_te, dec):
    b, t, d = x3.shape
    n = b * t
    x = x3.reshape(n, d)
    pool_w = weights["pool_scale"].shape[-1]
    conv_w = weights["conv_dw"].shape[-1]
    att_w = d - pool_w - conv_w
    n_heads = att_w // HEAD_DIM
    q_col = (pool_w + 2 * conv_w) // HEAD_DIM
    n_route = -(-n // LANES) * LANES
    new_k, new_v, new_pool, new_conv = [], [], [], []
    for l in range(DEPTH):
        if dec is None:
            mod3 = mod_rows[l].reshape(b, 1, N_MOD * d)
        else:
            mod3 = jnp.repeat(mod_rows[l], t, axis=0).reshape(1, n, N_MOD * d)
        u = _normproj(x, weights["norm1_g"][l], mod3, 1, 0, weights["w_in"][l], tm, tn, emit_h=False)
        k = u[:, q_col * HEAD_DIM + att_w:q_col * HEAD_DIM + 2 * att_w].reshape(b, t, att_w)
        v = u[:, q_col * HEAD_DIM + 2 * att_w:].reshape(b, t, att_w)
        mix_w = (weights["pool_w"][l], weights["pool_scale"][l], weights["conv_dw"][l], weights["conv_dw_b"][l],
                 weights["conv_ln_g"][l], weights["conv_ln_b"][l], weights["conv_pw"][l])
        gmix = weights["mix_norm_g"][l]
        if dec is None:
            u3 = u.reshape(b, t, -1)
            ya, yb, pst, cst = _mix(u3, None, None, mix_w, gmix, tt=tt, t_real=tt, pos0=0, halo_is_state=False)
            yc = _attention(u3, weights["sb_bias"][l], n_heads, q_col, q_col + n_heads, q_col + 2 * n_heads)
        else:
            u3 = jnp.pad(u.reshape(b, t, -1), ((0, 0), (0, t_real_pad - t), (0, 0)))
            a_halo = jnp.pad(dec["state_pool"][l], ((0, 0), (POOL_HALO - POOL_STATE, 0), (0, 0)))
            z_halo = jnp.pad(dec["state_conv"][l], ((0, 0), (CONV_HALO - CONV_STATE, 0), (0, 0)))
            ya, yb, pst, cst = _mix(u3, a_halo, z_halo, mix_w, gmix, tt=t_real_pad, t_real=t, pos0=pos0,
                                    halo_is_state=True)
            ya, yb = ya[:, :t], yb[:, :t]
            q = u[:, q_col * HEAD_DIM:q_col * HEAD_DIM + att_w].reshape(b, t, att_w)
            yc = _dec_attention(q, k, v, dec["cache_k"], dec["cache_v"], dec["page_table"], l,
                                weights["sb_bias"][l])
        x = _outproj(ya.reshape(n, pool_w), yb.reshape(n, conv_w), yc.reshape(n, att_w), gmix,
                     weights["w_out"][l], x, mod3, 2, tm, tn)
        pq, h2 = _normproj(x, weights["norm2_g"][l], mod3, 4, 3, weights["peer_wq"][l], tm, tn, emit_h=True)
        if n_route != n:
            pad = ((0, n_route - n), (0, 0))
            pq, h2p, xp = jnp.pad(pq, pad), jnp.pad(h2, pad), jnp.pad(x, pad)
            mod3p = jnp.pad(mod3, ((0, 0), (0, n_route - n), (0, 0)))
        else:
            h2p, xp, mod3p = h2, x, mod3
        s0t, s1t, tau, beta = _route(pq, weights["peer_subkeys"][l])
        x = _peer(h2p, weights["peer_u"][l], weights["peer_v"][l], s0t, s1t, tau, beta, xp, mod3p, 5,
                  peer_tm, peer_te)[:n]
        new_k.append(k)
        new_v.append(v)
        new_pool.append(pst)
        new_conv.append(cst)
    y = _final_norm(x, final_g, tm).reshape(b, t, d)
    heads = lambda arrs: jnp.stack(arrs, axis=1).reshape(b, DEPTH, t, n_heads, HEAD_DIM)
    return y, heads(new_k), heads(new_v), jnp.stack(new_pool, 0), jnp.stack(new_conv, 0)


def kernel(x_prompt, x_sample, c_prompt, c_sample, cache_k, cache_v, page_table, state_pool, state_conv, norm1_g, w_mod, b_mod, w_in, sb_bias, pool_w, pool_scale, conv_dw, conv_dw_b, conv_ln_g, conv_ln_b, conv_pw, mix_norm_g, w_out, norm2_g, peer_wq, peer_subkeys, peer_u, peer_v, final_g):
    bp, tp, d = x_prompt.shape
    bs, ts, _ = x_sample.shape
    weights = dict(norm1_g=norm1_g, w_in=w_in.astype(BF16), sb_bias=sb_bias, pool_w=pool_w.astype(BF16),
                   pool_scale=pool_scale, conv_dw=conv_dw, conv_dw_b=conv_dw_b, conv_ln_g=conv_ln_g,
                   conv_ln_b=conv_ln_b, conv_pw=conv_pw.astype(BF16), mix_norm_g=mix_norm_g,
                   w_out=w_out.astype(BF16), norm2_g=norm2_g, peer_wq=peer_wq.astype(BF16),
                   peer_subkeys=peer_subkeys, peer_u=peer_u.astype(BF16), peer_v=peer_v.astype(BF16))
    c_rows = bp + bs
    c_pad = -(-c_rows // SUBLANES) * SUBLANES
    c_all = jnp.pad(jnp.concatenate([c_prompt, c_sample], axis=0), ((0, c_pad - c_rows), (0, 0)))
    mod = _modulation(c_all, w_mod, b_mod)
    n_phys = cache_k.shape[0]
    dec = dict(cache_k=cache_k.reshape(n_phys, DEPTH, PAGE_SIZE, -1),
               cache_v=cache_v.reshape(n_phys, DEPTH, PAGE_SIZE, -1),
               page_table=page_table, state_pool=state_pool, state_conv=state_conv)
    y_p, k_p, v_p, pool_p, conv_p = _trunk(
        x_prompt, mod[:, :bp], weights, final_g, tm=512, tn=512, tt=256, t_real_pad=None, pos0=0,
        peer_tm=512, peer_te=256, dec=None)
    y_s, k_s, v_s, pool_s, conv_s = _trunk(
        x_sample, mod[:, bp:c_rows], weights, final_g, tm=bs * ts, tn=512, tt=None, t_real_pad=SUBLANES,
        pos0=page_table.shape[1] * PAGE_SIZE, peer_tm=LANES, peer_te=256, dec=dec)
    return (y_p, y_s, k_p, v_p, k_s, v_s, pool_p, pool_s, conv_p, conv_s)
```

```python
import functools

import jax
import jax.numpy as jnp
from jax import lax
from jax.experimental import pallas as pl
from jax.experimental.pallas import tpu as pltpu

F32 = jnp.float32
BF16 = jnp.bfloat16

SUBLANES = 8
LANES = 128
VMEM_LIMIT_MB = 56

DEPTH = 4
PAGE_SIZE = 128
POOL_WINDOWS = (2, 4, 8, 16)
POOL_STATE = 15
POOL_HALO = 16
CONV_K = 31
CONV_STATE = CONV_K - 1
CONV_HALO = 32
HEAD_DIM = 128
SB_SCALE = HEAD_DIM ** -0.5
PEER_HEADS = 8
PEER_HALF = 128
N_KEYS = 128
PEER_TOPK = 16
N_MOD = 6
EPS = 1e-6
DEC_Q_PAD = 8

_NT = (((1,), (1,)), ((), ()))


def _cparams(semantics):
    return pltpu.CompilerParams(dimension_semantics=semantics, vmem_limit_bytes=VMEM_LIMIT_MB << 20)


def _rms(x, g):
    return x * lax.rsqrt(jnp.mean(x * x, axis=-1, keepdims=True) + EPS) * g


def _mod_kernel(c_ref, w_ref, b_ref, o_ref):
    c = c_ref[...]
    a = (c * jax.nn.sigmoid(c)).astype(BF16)
    o_ref[...] = jnp.dot(a, w_ref[...].astype(BF16), preferred_element_type=F32) + b_ref[...]


def _modulation(c_all, w_mod, b_mod, tn=512):
    rows, d = c_all.shape
    depth, _, cols = w_mod.shape
    return pl.pallas_call(
        _mod_kernel,
        grid=(depth, cols // tn),
        in_specs=[
            pl.BlockSpec((rows, d), lambda l, j: (0, 0)),
            pl.BlockSpec((None, d, tn), lambda l, j: (l, 0, j)),
            pl.BlockSpec((None, 1, tn), lambda l, j: (l, 0, j)),
        ],
        out_specs=pl.BlockSpec((None, rows, tn), lambda l, j: (l, 0, j)),
        out_shape=jax.ShapeDtypeStruct((depth, rows, cols), F32),
        compiler_params=_cparams(("arbitrary", "arbitrary")),
        name="modulation",
    )(c_all, w_mod, b_mod.reshape(depth, 1, cols))


def _normproj_kernel(x_ref, g_ref, sc_ref, sh_ref, w_ref, o_ref, h_ref):
    @pl.when(pl.program_id(1) == 0)
    def _():
        h = _rms(x_ref[...], g_ref[...]) * (1.0 + sc_ref[...]) + sh_ref[...]
        h_ref[...] = h.astype(BF16)

    o_ref[...] = jnp.dot(h_ref[...], w_ref[...], preferred_element_type=F32)


def _normproj(x, g, mod3, k_scale, k_shift, w, tm, tn, emit_h):
    n, d = x.shape
    cols = w.shape[1]
    groups, r, _ = mod3.shape
    tiles_per_group = n // tm // groups
    mod_spec = lambda k: pl.BlockSpec((None, r, d), lambda i, j: (i // tiles_per_group, 0, k))
    in_specs = [
        pl.BlockSpec((tm, d), lambda i, j: (i, 0)),
        pl.BlockSpec((1, d), lambda i, j: (0, 0)),
        mod_spec(k_scale),
        mod_spec(k_shift),
        pl.BlockSpec((d, tn), lambda i, j: (0, j)),
    ]
    o_spec = pl.BlockSpec((tm, tn), lambda i, j: (i, j))
    o_shape = jax.ShapeDtypeStruct((n, cols), F32)
    if emit_h:
        out_specs = [o_spec, pl.BlockSpec((tm, d), lambda i, j: (i, 0))]
        out_shape = [o_shape, jax.ShapeDtypeStruct((n, d), BF16)]
        scratch = []
    else:
        out_specs, out_shape, scratch = o_spec, o_shape, [pltpu.VMEM((tm, d), BF16)]
    return pl.pallas_call(
        _normproj_kernel,
        grid=(n // tm, cols // tn),
        in_specs=in_specs,
        out_specs=out_specs,
        out_shape=out_shape,
        scratch_shapes=scratch,
        compiler_params=_cparams(("arbitrary", "arbitrary")),
        name="normproj_h" if emit_h else "normproj",
    )(x, g.reshape(1, d), mod3, mod3, w)


def _tri_ones(tk):
    r = lax.broadcasted_iota(jnp.int32, (tk, tk + LANES), 0)
    c = lax.broadcasted_iota(jnp.int32, (tk, tk + LANES), 1)
    return jnp.where(c >= tk, 1.0, jnp.where(r > c, 1.0, 0.0)).astype(BF16)


def _sb_block(z, run, tri, valid):
    tk = z.shape[1]
    sp = jnp.maximum(z, 0.0) + jnp.log(1.0 + jnp.exp(-jnp.abs(z)))
    l1m = -sp if valid is None else jnp.where(valid, -sp, 0.0)
    sr = jnp.dot(l1m.astype(BF16), tri, preferred_element_type=F32)
    run_b = run if tk == LANES else jnp.concatenate([run] * (tk // LANES), axis=-1)
    a = jnp.exp((z - sp) + sr[:, :tk] + run_b)
    if valid is not None:
        a = jnp.where(valid, a, 0.0)
    return a, run + sr[:, tk:]


def _attn_kernel(bias_ref, q_ref, k_ref, v_ref, o_ref, q_scr, k_scr, v_scr, *, blk, hb):
    t = q_ref.shape[0]
    biases = [bias_ref[pl.program_id(1) * hb + i] for i in range(hb)]
    q_scr[...] = (q_ref[...] * SB_SCALE).astype(BF16)
    k_scr[...] = k_ref[...].astype(BF16)
    v_scr[...] = v_ref[...].astype(BF16)
    tri = _tri_ones(blk)
    row = lax.broadcasted_iota(jnp.int32, (blk, blk), 0)
    col = lax.broadcasted_iota(jnp.int32, (blk, blk), 1)
    diag_valid = col < row

    def q_block(qi, _):
        q0 = pl.multiple_of(qi * blk, blk)

        def k_block(k0, carry, valid):
            new = []
            for i, (acc, run) in enumerate(carry):
                ls = slice(i * HEAD_DIM, (i + 1) * HEAD_DIM)
                z = lax.dot_general(q_scr[pl.ds(q0, blk), ls], k_scr[pl.ds(k0, blk), ls], _NT,
                                    preferred_element_type=F32) + biases[i]
                a, run = _sb_block(z, run, tri, valid)
                acc = acc + jnp.dot(a.astype(BF16), v_scr[pl.ds(k0, blk), ls], preferred_element_type=F32)
                new.append((acc, run))
            return tuple(new)

        carry = ((jnp.zeros((blk, HEAD_DIM), F32), jnp.zeros((blk, LANES), F32)),) * hb
        carry = k_block(q0, carry, diag_valid)

        def body(s, carry):
            k0 = pl.multiple_of((qi - 1 - s) * blk, blk)
            return k_block(k0, carry, None)

        carry = lax.fori_loop(0, qi, body, carry)
        for i, (acc, _) in enumerate(carry):
            o_ref[pl.ds(q0, blk), i * HEAD_DIM:(i + 1) * HEAD_DIM] = acc
        return 0

    lax.fori_loop(0, t // blk, q_block, 0)


def _attention(u3, sb_bias, n_heads, q_col, k_col, v_col, blk=256, hb=4):
    b, t, _ = u3.shape
    assert n_heads % hb == 0 and q_col % hb == 0 and k_col % hb == 0 and v_col % hb == 0
    spec = lambda c0: pl.BlockSpec((None, t, hb * HEAD_DIM), lambda i, h: (i, 0, c0 // hb + h))
    return pl.pallas_call(
        functools.partial(_attn_kernel, blk=blk, hb=hb),
        grid=(b, n_heads // hb),
        in_specs=[pl.BlockSpec(memory_space=pltpu.SMEM), spec(q_col), spec(k_col), spec(v_col)],
        out_specs=pl.BlockSpec((None, t, hb * HEAD_DIM), lambda i, h: (i, 0, h)),
        out_shape=jax.ShapeDtypeStruct((b, t, n_heads * HEAD_DIM), F32),
        scratch_shapes=[pltpu.VMEM((t, hb * HEAD_DIM), BF16)] * 3,
        compiler_params=_cparams(("arbitrary", "arbitrary")),
        name="attn",
    )(sb_bias, u3, u3, u3)


def _dec_attn_kernel(pt_ref, q_ref, bias_ref, kn_ref, vn_ref, *refs, n_heads, n_sub):
    kc_refs, vc_refs = refs[:n_sub], refs[n_sub:2 * n_sub]
    o_ref, q_scr, acc_scr, run_scr = refs[2 * n_sub:]
    j = pl.program_id(1)
    rows = n_heads * DEC_Q_PAD
    head_rows = lambda h: slice(h * DEC_Q_PAD, (h + 1) * DEC_Q_PAD)
    head_cols = lambda h: slice(h * HEAD_DIM, (h + 1) * HEAD_DIM)

    @pl.when(j == 0)
    def _():
        for h in range(n_heads):
            q_scr[head_rows(h), :] = q_ref[:, head_cols(h)] * SB_SCALE
        acc_scr[...] = jnp.zeros_like(acc_scr)
        run_scr[...] = jnp.zeros_like(run_scr)

    tri = _tri_ones(PAGE_SIZE)
    of_head = lambda ref, h: ref[pl.ds(h, PAGE_SIZE, stride=n_heads), :].astype(BF16)

    def page(k_ref, v_ref, carry, valid):
        acc, run = carry
        z = jnp.concatenate(
            [lax.dot_general(q_scr[head_rows(h), :].astype(BF16), of_head(k_ref, h), _NT,
                             preferred_element_type=F32) for h in range(n_heads)], axis=0) + bias_ref[...]
        a, run = _sb_block(z, run, tri, valid)
        acc = acc + jnp.concatenate(
            [jnp.dot(a[head_rows(h), :].astype(BF16), of_head(v_ref, h), preferred_element_type=F32)
             for h in range(n_heads)], axis=0)
        return acc, run

    @pl.when(j == 0)
    def _():
        qi = jnp.bitwise_and(lax.broadcasted_iota(jnp.int32, (rows, PAGE_SIZE), 0), DEC_Q_PAD - 1)
        kc = lax.broadcasted_iota(jnp.int32, (rows, PAGE_SIZE), 1)
        acc, run = page(kn_ref, vn_ref, (acc_scr[...], run_scr[...]), kc < qi)
        acc_scr[...] = acc
        run_scr[...] = run

    @pl.when(j > 0)
    def _():
        carry = (acc_scr[...], run_scr[...])
        for s in range(n_sub):
            carry = page(kc_refs[s], vc_refs[s], carry, None)
        acc_scr[...] = carry[0]
        run_scr[...] = carry[1]

    @pl.when(j == pl.num_programs(1) - 1)
    def _():
        for h in range(n_heads):
            o_ref[:, head_cols(h)] = acc_scr[head_rows(h), :]


def _dec_attention(q, k_new, v_new, cache_k, cache_v, page_table, layer, sb_bias, n_sub=4):
    db, ts, width = q.shape
    n_heads = width // HEAD_DIM
    n_pages = page_table.shape[1]
    assert n_pages % n_sub == 0
    rows = n_heads * DEC_Q_PAD
    page_rows = PAGE_SIZE * n_heads
    q_pad = jnp.pad(q, ((0, 0), (0, DEC_Q_PAD - ts), (0, 0)))
    as_page = lambda x: jnp.pad(x.reshape(db, ts * n_heads, HEAD_DIM), ((0, 0), (0, page_rows - ts * n_heads), (0, 0)))
    bias = jnp.broadcast_to(jnp.repeat(sb_bias, DEC_Q_PAD)[:, None], (rows, PAGE_SIZE))
    per_seq = lambda r, w: pl.BlockSpec((None, r, w), lambda b, j, pt: (b, 0, 0))

    def cache_spec(s):
        page = lambda b, j, pt: pt[b, n_pages - 1 - (jnp.maximum(j - 1, 0) * n_sub + s)]
        return pl.BlockSpec((None, None, page_rows, HEAD_DIM), lambda b, j, pt: (page(b, j, pt), layer, 0, 0))

    out = pl.pallas_call(
        functools.partial(_dec_attn_kernel, n_heads=n_heads, n_sub=n_sub),
        grid_spec=pltpu.PrefetchScalarGridSpec(
            num_scalar_prefetch=1,
            grid=(db, n_pages // n_sub + 1),
            in_specs=[per_seq(DEC_Q_PAD, width), pl.BlockSpec((rows, PAGE_SIZE), lambda b, j, pt: (0, 0)),
                      per_seq(page_rows, HEAD_DIM), per_seq(page_rows, HEAD_DIM)]
            + [cache_spec(s) for s in range(n_sub)] * 2,
            out_specs=per_seq(DEC_Q_PAD, width),
            scratch_shapes=[pltpu.VMEM((rows, HEAD_DIM), F32), pltpu.VMEM((rows, HEAD_DIM), F32),
                            pltpu.VMEM((rows, LANES), F32)],
        ),
        out_shape=jax.ShapeDtypeStruct((db, DEC_Q_PAD, width), F32),
        compiler_params=_cparams(("arbitrary", "arbitrary")),
        name="dec_attn",
    )(page_table, q_pad, bias, as_page(k_new), as_page(v_new), *([cache_k] * n_sub), *([cache_v] * n_sub))
    return out[:, :ts]


def _mix_kernel(a_ref, ah_ref, val_ref, gate_ref, zh0_ref, zh1_ref, pw_ref, ps_ref, dw_ref, dwb_ref, lng_ref,
                lnb_ref, cpw_ref, gm_ref, ya_ref, yb_ref, pst_ref, cst_ref, fa_scr, fz_scr, y_scr,
                *, t_real, pos0, halo_is_state):
    t = pl.program_id(1)
    tt, pool_w = a_ref.shape
    conv_w = val_ref.shape[1]
    group_w = pool_w // len(POOL_WINDOWS)

    a = a_ref[...]
    ah = ah_ref[...]
    if not halo_is_state:
        ah = jnp.where(t == 0, 0.0, ah)
    fa_scr[0:POOL_HALO, :] = ah
    fa_scr[POOL_HALO:, :] = a

    @pl.when(t == pl.num_programs(1) - 1)
    def _():
        end = POOL_HALO + t_real
        pst_ref[...] = fa_scr[end - POOL_STATE:end, :]

    pos = pos0 + t * tt + lax.broadcasted_iota(jnp.int32, (tt, group_w), 0)
    outs = []
    for gi, win in enumerate(POOL_WINDOWS):
        cs = slice(gi * group_w, (gi + 1) * group_w)
        acc = fa_scr[POOL_HALO:POOL_HALO + tt, cs]
        for w in range(1, win):
            acc = acc + fa_scr[POOL_HALO - w:POOL_HALO - w + tt, cs]
        cnt = jnp.minimum(win, pos + 1).astype(F32)
        d = acc / cnt - a[:, cs]
        outs.append(jnp.dot(d.astype(BF16), pw_ref[gi], preferred_element_type=F32))
    ya = jnp.concatenate(outs, axis=-1) * ps_ref[...]
    ya_ref[...] = _rms(ya, gm_ref[:, 0:pool_w])

    if halo_is_state:
        zh = zh0_ref[...]
    else:
        zh = jnp.where(t == 0, 0.0, zh0_ref[...] * jax.nn.sigmoid(zh1_ref[...]))
    fz_scr[0:CONV_HALO, :] = zh
    fz_scr[CONV_HALO:, :] = val_ref[...] * jax.nn.sigmoid(gate_ref[...])

    @pl.when(t == pl.num_programs(1) - 1)
    def _():
        end = CONV_HALO + t_real
        cst_ref[...] = fz_scr[end - CONV_STATE:end, :]

    base = CONV_HALO - CONV_STATE
    for c in range(conv_w // LANES):
        cs = slice(c * LANES, (c + 1) * LANES)
        acc = jnp.zeros((tt, LANES), F32) + dwb_ref[:, cs]
        for lo in range(SUBLANES):
            n_hi = len(range(lo, CONV_K, SUBLANES))
            win = fz_scr[base + lo:base + lo + tt + SUBLANES * (n_hi - 1), cs]
            for hi in range(n_hi):
                j = SUBLANES * hi + lo
                acc = acc + dw_ref[j:j + 1, cs] * win[SUBLANES * hi:SUBLANES * hi + tt]
        y_scr[:, cs] = acc
    y = y_scr[...]
    mu = jnp.mean(y, axis=-1, keepdims=True)
    yc = y - mu
    var = jnp.mean(yc * yc, axis=-1, keepdims=True)
    yn = yc * lax.rsqrt(var + EPS) * lng_ref[...] + lnb_ref[...]
    s = yn * jax.nn.sigmoid(yn)
    yb = jnp.dot(s.astype(BF16), cpw_ref[...], preferred_element_type=F32)
    yb_ref[...] = _rms(yb, gm_ref[:, pool_w:pool_w + conv_w])


def _mix(u3, a_halo, z_halo, weights, gmix, *, tt, t_real, pos0, halo_is_state):
    b, t, _ = u3.shape
    pool_w16, pool_scale, conv_dw, conv_dw_b, ln_g, ln_b, conv_pw16 = weights
    w = pool_scale.shape[-1]
    nt = t // tt
    tile = lambda c: pl.BlockSpec((None, tt, w), lambda i, s: (i, s, c))
    if halo_is_state:
        ah_spec = pl.BlockSpec((None, POOL_HALO, w), lambda i, s: (i, 0, 0))
        zh_specs = [pl.BlockSpec((None, CONV_HALO, w), lambda i, s: (i, 0, 0))] * 2
        halos = (a_halo, z_halo, z_halo)
    else:
        prev = lambda rows, c: pl.BlockSpec(
            (None, rows, w), lambda i, s: (i, jnp.maximum(s * (tt // rows) - 1, 0), c))
        ah_spec = prev(POOL_HALO, 0)
        zh_specs = [prev(CONV_HALO, 1), prev(CONV_HALO, 2)]
        halos = (u3, u3, u3)
    full = lambda shape: pl.BlockSpec(shape, lambda i, s: (0,) * len(shape))
    row = lambda x: x.reshape(1, -1)
    return pl.pallas_call(
        functools.partial(_mix_kernel, t_real=t_real, pos0=pos0, halo_is_state=halo_is_state),
        grid=(b, nt),
        in_specs=[tile(0), ah_spec, tile(1), tile(2), *zh_specs,
                  full(pool_w16.shape), full((1, w)), full(conv_dw.shape), full((1, w)), full((1, w)),
                  full((1, w)), full(conv_pw16.shape), full((1, gmix.shape[-1]))],
        out_specs=[pl.BlockSpec((None, tt, w), lambda i, s: (i, s, 0))] * 2
        + [pl.BlockSpec((None, POOL_STATE, w), lambda i, s: (i, 0, 0)),
           pl.BlockSpec((None, CONV_STATE, w), lambda i, s: (i, 0, 0))],
        out_shape=[jax.ShapeDtypeStruct((b, t, w), F32)] * 2
        + [jax.ShapeDtypeStruct((b, POOL_STATE, w), F32), jax.ShapeDtypeStruct((b, CONV_STATE, w), F32)],
        scratch_shapes=[pltpu.VMEM((POOL_HALO + tt, w), F32), pltpu.VMEM((CONV_HALO + tt, w), F32),
                        pltpu.VMEM((tt, w), F32)],
        compiler_params=_cparams(("arbitrary", "arbitrary")),
        name="mix",
    )(u3, halos[0], u3, u3, halos[1], halos[2], pool_w16, row(pool_scale), conv_dw, row(conv_dw_b), row(ln_g),
      row(ln_b), conv_pw16, row(gmix))


def _outproj_kernel(ya_ref, yb_ref, yc_ref, gm_ref, w_ref, x_ref, g1_ref, o_ref, m_scr):
    @pl.when(pl.program_id(1) == 0)
    def _():
        wa, wb = ya_ref.shape[1], yb_ref.shape[1]
        m_scr[:, 0:wa] = ya_ref[...].astype(BF16)
        m_scr[:, wa:wa + wb] = yb_ref[...].astype(BF16)
        m_scr[:, wa + wb:] = _rms(yc_ref[...], gm_ref[:, wa + wb:]).astype(BF16)

    o_ref[...] = x_ref[...] + g1_ref[...] * jnp.dot(m_scr[...], w_ref[...], preferred_element_type=F32)


def _outproj(ya, yb, yc, gmix, w, x, mod3, k_gate, tm, tn):
    n, d = x.shape
    groups, r, _ = mod3.shape
    tiles_per_group = n // tm // groups
    rows = lambda width: pl.BlockSpec((tm, width), lambda i, j: (i, 0))
    return pl.pallas_call(
        _outproj_kernel,
        grid=(n // tm, d // tn),
        in_specs=[rows(ya.shape[1]), rows(yb.shape[1]), rows(yc.shape[1]),
                  pl.BlockSpec((1, d), lambda i, j: (0, 0)),
                  pl.BlockSpec((d, tn), lambda i, j: (0, j)),
                  pl.BlockSpec((tm, tn), lambda i, j: (i, j)),
                  pl.BlockSpec((None, r, tn), lambda i, j: (i // tiles_per_group, 0, k_gate * (d // tn) + j))],
        out_specs=pl.BlockSpec((tm, tn), lambda i, j: (i, j)),
        out_shape=jax.ShapeDtypeStruct((n, d), F32),
        scratch_shapes=[pltpu.VMEM((tm, d), BF16)],
        compiler_params=_cparams(("arbitrary", "arbitrary")),
        name="outproj",
    )(ya, yb, yc, gmix.reshape(1, d), w, x, mod3)


def _sort_network(n):
    pairs = []

    def merge(lo, hi, r):
        step = r * 2
        if step < hi - lo:
            merge(lo, hi, step)
            merge(lo + r, hi, step)
            pairs.extend((i, i + r) for i in range(lo + r, hi - r, step))
        else:
            pairs.append((lo, lo + r))

    def sort(lo, hi):
        if hi - lo >= 1:
            mid = lo + (hi - lo) // 2
            sort(lo, mid)
            sort(mid + 1, hi)
            merge(lo, hi, 1)

    sort(0, n - 1)
    return pairs


_SORT16 = _sort_network(PEER_TOPK)
_BITONIC16 = [(i, i + d) for d in (8, 4, 2, 1) for i in range(PEER_TOPK) if not i & d]


def _exchange(v, pairs):
    for i, j in pairs:
        v[i], v[j] = jnp.maximum(v[i], v[j]), jnp.minimum(v[i], v[j])
    return v


def _merge_sublanes(v):
    for shift in (4, 2, 1):
        other = [pltpu.roll(x, shift, 0) for x in v]
        v = [jnp.maximum(v[i], other[PEER_TOPK - 1 - i]) for i in range(PEER_TOPK)]
        v = _exchange(v, _BITONIC16)
    return v


def _top16_keys(s):
    v = [s[SUBLANES * i:SUBLANES * (i + 1), :] for i in range(N_KEYS // SUBLANES)]
    return _merge_sublanes(_exchange(v, _SORT16))


def _route_kernel(q_ref, sk_ref, s0_ref, s1_ref, tau_ref, beta_ref):
    sub = lax.broadcasted_iota(jnp.int32, (SUBLANES, LANES), 0)
    sk = [sk_ref[p].astype(BF16) for p in range(2)]
    tau_all = jnp.zeros((SUBLANES, LANES), F32)
    beta_all = jnp.zeros((SUBLANES, LANES), F32)
    for h in range(PEER_HEADS):
        top = []
        for p, s_ref in enumerate((s0_ref, s1_ref)):
            c0 = (2 * h + p) * PEER_HALF
            qc = q_ref[:, c0:c0 + PEER_HALF].astype(BF16)
            s = lax.dot_general(sk[p], qc, _NT, preferred_element_type=F32)
            s_ref[h] = s
            top.append(_top16_keys(s))
        a, b = top
        b_lo, b_hi = b[SUBLANES - 1], b[2 * SUBLANES - 1]
        for r in range(SUBLANES - 2, -1, -1):
            b_lo = jnp.where(sub == r, b[r], b_lo)
            b_hi = jnp.where(sub == r, b[SUBLANES + r], b_hi)
        cand = _exchange([x + b_lo for x in a], _SORT16)
        extra = a[0] + b_hi
        for i in range(PEER_TOPK):
            cand[i], extra = jnp.maximum(cand[i], extra), jnp.minimum(cand[i], extra)
        tv = _merge_sublanes(cand)
        zsum = jnp.ones((SUBLANES, LANES), F32)
        for r in range(1, PEER_TOPK):
            zsum = zsum + jnp.exp(tv[r] - tv[0])
        tau_all = jnp.where(sub == h, tv[PEER_TOPK - 1], tau_all)
        beta_all = jnp.where(sub == h, -(tv[0] + jnp.log(zsum)), beta_all)
    tau_ref[...] = tau_all
    beta_ref[...] = beta_all


def _route(q, subkeys):
    n, qc = q.shape
    score = pl.BlockSpec((PEER_HEADS, N_KEYS, LANES), lambda i: (0, 0, i))
    head_row = pl.BlockSpec((PEER_HEADS, LANES), lambda i: (0, i))
    return pl.pallas_call(
        _route_kernel,
        grid=(n // LANES,),
        in_specs=[pl.BlockSpec((LANES, qc), lambda i: (i, 0)),
                  pl.BlockSpec(subkeys.shape, lambda i: (0, 0, 0))],
        out_specs=[score, score, head_row, head_row],
        out_shape=[jax.ShapeDtypeStruct((PEER_HEADS, N_KEYS, n), F32)] * 2
        + [jax.ShapeDtypeStruct((PEER_HEADS, n), F32)] * 2,
        compiler_params=_cparams(("arbitrary",)),
        name="route",
    )(q, subkeys)


def _gelu_tanh(x):
    return 0.5 * x * (1.0 + jnp.tanh(0.7978845608028654 * (x + 0.044715 * (x * x * x))))


def _peer_kernel(h_ref, u_ref, v_ref, s0_ref, s1_ref, tau_ref, beta_ref, x_ref, g2_ref, o_ref, w_scr, *, sub):
    t = pl.program_id(1)
    te = u_ref.shape[0]
    tm = h_ref.shape[0]

    @pl.when(t == 0)
    def _():
        o_ref[...] = jnp.zeros_like(o_ref)

    h = h_ref[...]
    hts = [lax.dot_general(u_ref[s * sub:(s + 1) * sub, :], h, _NT, preferred_element_type=F32)
           for s in range(te // sub)]
    acc = o_ref[...]
    for s, ht in enumerate(hts):
        for ii in range(sub // N_KEYS):
            rows = slice(ii * N_KEYS, (ii + 1) * N_KEYS)
            for cc in range(tm // LANES):
                ls = slice(cc * LANES, (cc + 1) * LANES)
                gate = jnp.zeros((N_KEYS, LANES), F32)
                for hd in range(PEER_HEADS):
                    p = s0_ref[s * (sub // N_KEYS) + ii, hd:hd + 1, ls] + s1_ref[hd, :, ls]
                    gate = gate + jnp.where(p >= tau_ref[hd:hd + 1, ls], jnp.exp(p + beta_ref[hd:hd + 1, ls]), 0.0)
                w_scr[s, rows, ls] = _gelu_tanh(ht[rows, ls]) * gate
        w = w_scr[s].T.astype(BF16)
        acc = acc + jnp.dot(w, v_ref[s * sub:(s + 1) * sub, :], preferred_element_type=F32)
    o_ref[...] = acc

    @pl.when(t == pl.num_programs(1) - 1)
    def _():
        o_ref[...] = x_ref[...] + g2_ref[...] * o_ref[...]


def _peer(h, u16, v16, s0t, s1t, tau, beta, x, mod3, k_gate, tm, te, sub=256):
    n, d = x.shape
    n_exp = u16.shape[0]
    groups, r, _ = mod3.shape
    tiles_per_group = n // tm // groups
    once = dict(pipeline_mode=pl.Buffered(1))
    return pl.pallas_call(
        functools.partial(_peer_kernel, sub=sub),
        grid=(n // tm, n_exp // te),
        in_specs=[pl.BlockSpec((tm, d), lambda i, t: (i, 0), **once),
                  pl.BlockSpec((te, d), lambda i, t: (t, 0)),
                  pl.BlockSpec((te, d), lambda i, t: (t, 0)),
                  pl.BlockSpec((te // N_KEYS, PEER_HEADS, tm), lambda i, t: (t, 0, i)),
                  pl.BlockSpec((PEER_HEADS, N_KEYS, tm), lambda i, t: (0, 0, i), **once),
                  pl.BlockSpec((PEER_HEADS, tm), lambda i, t: (0, i)),
                  pl.BlockSpec((PEER_HEADS, tm), lambda i, t: (0, i)),
                  pl.BlockSpec((tm, d), lambda i, t: (i, 0), **once),
                  pl.BlockSpec((None, r, d), lambda i, t: (i // tiles_per_group, 0, k_gate))],
        out_specs=pl.BlockSpec((tm, d), lambda i, t: (i, 0)),
        out_shape=jax.ShapeDtypeStruct((n, d), F32),
        scratch_shapes=[pltpu.VMEM((te // sub, sub, tm), F32)],
        compiler_params=_cparams(("arbitrary", "arbitrary")),
        name="peer",
    )(h, u16, v16, jnp.swapaxes(s0t, 0, 1), s1t, tau, beta, x, mod3)


def _final_norm_kernel(x_ref, g_ref, o_ref):
    o_ref[...] = _rms(x_ref[...], g_ref[...])


def _final_norm(x, g, tm):
    n, d = x.shape
    return pl.pallas_call(
        _final_norm_kernel,
        grid=(n // tm,),
        in_specs=[pl.BlockSpec((tm, d), lambda i: (i, 0)), pl.BlockSpec((1, d), lambda i: (0, 0))],
        out_specs=pl.BlockSpec((tm, d), lambda i: (i, 0)),
        out_shape=jax.ShapeDtypeStruct((n, d), F32),
        compiler_params=_cparams(("arbitrary",)),
        name="final_norm",
    )(x, g.reshape(1, d))


def _trunk(x3, mod_rows, weights, final_g, *, tm, tn, tt, t_real_pad, pos0, peer_tm, peer_te, dec):
    b, t, d = x3.shape
    n = b * t
    x = x3.reshape(n, d)
    pool_w = weights["pool_scale"].shape[-1]
    conv_w = weights["conv_dw"].shape[-1]
    att_w = d - pool_w - conv_w
    n_heads = att_w // HEAD_DIM
    q_col = (pool_w + 2 * conv_w) // HEAD_DIM
    n_route = -(-n // LANES) * LANES
    new_k, new_v, new_pool, new_conv = [], [], [], []
    for l in range(DEPTH):
        if dec is None:
            mod3 = mod_rows[l].reshape(b, 1, N_MOD * d)
        else:
            mod3 = jnp.repeat(mod_rows[l], t, axis=0).reshape(1, n, N_MOD * d)
        u = _normproj(x, weights["norm1_g"][l], mod3, 1, 0, weights["w_in"][l], tm, tn, emit_h=False)
        k = u[:, q_col * HEAD_DIM + att_w:q_col * HEAD_DIM + 2 * att_w].reshape(b, t, att_w)
        v = u[:, q_col * HEAD_DIM + 2 * att_w:].reshape(b, t, att_w)
        mix_w = (weights["pool_w"][l], weights["pool_scale"][l], weights["conv_dw"][l], weights["conv_dw_b"][l],
                 weights["conv_ln_g"][l], weights["conv_ln_b"][l], weights["conv_pw"][l])
        gmix = weights["mix_norm_g"][l]
        if dec is None:
            u3 = u.reshape(b, t, -1)
            ya, yb, pst, cst = _mix(u3, None, None, mix_w, gmix, tt=tt, t_real=tt, pos0=0, halo_is_state=False)
            yc = _attention(u3, weights["sb_bias"][l], n_heads, q_col, q_col + n_heads, q_col + 2 * n_heads)
        else:
            u3 = jnp.pad(u.reshape(b, t, -1), ((0, 0), (0, t_real_pad - t), (0, 0)))
            a_halo = jnp.pad(dec["state_pool"][l], ((0, 0), (POOL_HALO - POOL_STATE, 0), (0, 0)))
            z_halo = jnp.pad(dec["state_conv"][l], ((0, 0), (CONV_HALO - CONV_STATE, 0), (0, 0)))
            ya, yb, pst, cst = _mix(u3, a_halo, z_halo, mix_w, gmix, tt=t_real_pad, t_real=t, pos0=pos0,
                                    halo_is_state=True)
            ya, yb = ya[:, :t], yb[:, :t]
            q = u[:, q_col * HEAD_DIM:q_col * HEAD_DIM + att_w].reshape(b, t, att_w)
            yc = _dec_attention(q, k, v, dec["cache_k"], dec["cache_v"], dec["page_table"], l,
                                weights["sb_bias"][l])
        x = _outproj(ya.reshape(n, pool_w), yb.reshape(n, conv_w), yc.reshape(n, att_w), gmix,
                     weights["w_out"][l], x, mod3, 2, tm, tn)
        pq, h2 = _normproj(x, weights["norm2_g"][l], mod3, 4, 3, weights["peer_wq"][l], tm, tn, emit_h=True)
        if n_route != n:
            pad = ((0, n_route - n), (0, 0))
            pq, h2p, xp = jnp.pad(pq, pad), jnp.pad(h2, pad), jnp.pad(x, pad)
            mod3p = jnp.pad(mod3, ((0, 0), (0, n_route - n), (0, 0)))
        else:
            h2p, xp, mod3p = h2, x, mod3
        s0t, s1t, tau, beta = _route(pq, weights["peer_subkeys"][l])
        x = _peer(h2p, weights["peer_u"][l], weights["peer_v"][l], s0t, s1t, tau, beta, xp, mod3p, 5,
                  peer_tm, peer_te)[:n]
        new_k.append(k)
        new_v.append(v)
        new_pool.append(pst)
        new_conv.append(cst)
    y = _final_norm(x, final_g, tm).reshape(b, t, d)
    heads = lambda arrs: jnp.stack(arrs, axis=1).reshape(b, DEPTH, t, n_heads, HEAD_DIM)
    return y, heads(new_k), heads(new_v), jnp.stack(new_pool, 0), jnp.stack(new_conv, 0)


def kernel(x_prompt, x_sample, c_prompt, c_sample, cache_k, cache_v, page_table, state_pool, state_conv, norm1_g, w_mod, b_mod, w_in, sb_bias, pool_w, pool_scale, conv_dw, conv_dw_b, conv_ln_g, conv_ln_b, conv_pw, mix_norm_g, w_out, norm2_g, peer_wq, peer_subkeys, peer_u, peer_v, final_g):
    bp, tp, d = x_prompt.shape
    bs, ts, _ = x_sample.shape
    weights = dict(norm1_g=norm1_g, w_in=w_in.astype(BF16), sb_bias=sb_bias, pool_w=pool_w.astype(BF16),
                   pool_scale=pool_scale, conv_dw=conv_dw, conv_dw_b=conv_dw_b, conv_ln_g=conv_ln_g,
                   conv_ln_b=conv_ln_b, conv_pw=conv_pw.astype(BF16), mix_norm_g=mix_norm_g,
                   w_out=w_out.astype(BF16), norm2_g=norm2_g, peer_wq=peer_wq.astype(BF16),
                   peer_subkeys=peer_subkeys, peer_u=peer_u.astype(BF16), peer_v=peer_v.astype(BF16))
    c_rows = bp + bs
    c_pad = -(-c_rows // SUBLANES) * SUBLANES
    c_all = jnp.pad(jnp.concatenate([c_prompt, c_sample], axis=0), ((0, c_pad - c_rows), (0, 0)))
    mod = _modulation(c_all, w_mod, b_mod)
    n_phys = cache_k.shape[0]
    dec = dict(cache_k=cache_k.reshape(n_phys, DEPTH, -1, HEAD_DIM),
               cache_v=cache_v.reshape(n_phys, DEPTH, -1, HEAD_DIM),
               page_table=page_table, state_pool=state_pool, state_conv=state_conv)
    y_p, k_p, v_p, pool_p, conv_p = _trunk(
        x_prompt, mod[:, :bp], weights, final_g, tm=512, tn=512, tt=256, t_real_pad=None, pos0=0,
        peer_tm=512, peer_te=512, dec=None)
    y_s, k_s, v_s, pool_s, conv_s = _trunk(
        x_sample, mod[:, bp:c_rows], weights, final_g, tm=bs * ts, tn=512, tt=None, t_real_pad=SUBLANES,
        pos0=page_table.shape[1] * PAGE_SIZE, peer_tm=LANES, peer_te=512, dec=dec)
    return (y_p, y_s, k_p, v_p, k_s, v_s, pool_p, pool_s, conv_p, conv_s)
```

```python
import functools

import jax
import jax.numpy as jnp
from jax import lax
from jax.experimental import pallas as pl
from jax.experimental.pallas import tpu as pltpu

F32 = jnp.float32
BF16 = jnp.bfloat16

SUBLANES = 8
LANES = 128
VMEM_LIMIT_MB = 56

DEPTH = 4
PAGE_SIZE = 128
POOL_WINDOWS = (2, 4, 8, 16)
POOL_STATE = 15
POOL_HALO = 16
CONV_K = 31
CONV_STATE = CONV_K - 1
CONV_HALO = 32
HEAD_DIM = 128
SB_SCALE = HEAD_DIM ** -0.5
PEER_HEADS = 8
PEER_HALF = 128
N_KEYS = 128
PEER_TOPK = 16
N_MOD = 6
EPS = 1e-6
DEC_Q_PAD = 8

_NT = (((1,), (1,)), ((), ()))


def _cparams(semantics):
    return pltpu.CompilerParams(dimension_semantics=semantics, vmem_limit_bytes=VMEM_LIMIT_MB << 20)


def _rms(x, g):
    return x * lax.rsqrt(jnp.mean(x * x, axis=-1, keepdims=True) + EPS) * g


def _mod_kernel(c_ref, w_ref, b_ref, o_ref):
    c = c_ref[...]
    a = (c * jax.nn.sigmoid(c)).astype(BF16)
    o_ref[...] = jnp.dot(a, w_ref[...].astype(BF16), preferred_element_type=F32) + b_ref[...]


def _modulation(c_all, w_mod, b_mod, tn=512):
    rows, d = c_all.shape
    depth, _, cols = w_mod.shape
    return pl.pallas_call(
        _mod_kernel,
        grid=(depth, cols // tn),
        in_specs=[
            pl.BlockSpec((rows, d), lambda l, j: (0, 0)),
            pl.BlockSpec((None, d, tn), lambda l, j: (l, 0, j)),
            pl.BlockSpec((None, 1, tn), lambda l, j: (l, 0, j)),
        ],
        out_specs=pl.BlockSpec((None, rows, tn), lambda l, j: (l, 0, j)),
        out_shape=jax.ShapeDtypeStruct((depth, rows, cols), F32),
        compiler_params=_cparams(("arbitrary", "arbitrary")),
        name="modulation",
    )(c_all, w_mod, b_mod.reshape(depth, 1, cols))


def _normproj_kernel(x_ref, g_ref, sc_ref, sh_ref, w_ref, *rest, n_alias, kv_tiles):
    rest = rest[n_alias:]
    o_ref, h_ref = rest[0], rest[-1]
    j = pl.program_id(1)

    @pl.when(j == 0)
    def _():
        h = _rms(x_ref[...], g_ref[...]) * (1.0 + sc_ref[...]) + sh_ref[...]
        h_ref[...] = h.astype(BF16)

    res = jnp.dot(h_ref[...], w_ref[...], preferred_element_type=F32)
    o_ref[...] = res
    if kv_tiles is not None:
        k0, v0, nkv = kv_tiles
        k_ref, v_ref = rest[1], rest[2]

        @pl.when((j >= k0) & (j < k0 + nkv))
        def _():
            k_ref[...] = res

        @pl.when((j >= v0) & (j < v0 + nkv))
        def _():
            v_ref[...] = res


def _normproj(x, g, mod3, k_scale, k_shift, w, tm, tn, emit_h, kv=None):
    n, d = x.shape
    cols = w.shape[1]
    groups, r, _ = mod3.shape
    tiles_per_group = n // tm // groups
    mod_spec = lambda k: pl.BlockSpec((None, r, d), lambda i, j: (i // tiles_per_group, 0, k))
    in_specs = [
        pl.BlockSpec((tm, d), lambda i, j: (i, 0)),
        pl.BlockSpec((1, d), lambda i, j: (0, 0)),
        mod_spec(k_scale),
        mod_spec(k_shift),
        pl.BlockSpec((d, tn), lambda i, j: (0, j)),
    ]
    args = [x, g.reshape(1, d), mod3, mod3, w]
    out_specs = [pl.BlockSpec((tm, tn), lambda i, j: (i, j))]
    out_shape = [jax.ShapeDtypeStruct((n, cols), F32)]
    n_alias, kv_tiles, aliases = 0, None, {}
    if kv is not None:
        layer, k_col0, v_col0, width, seq_len, n_layers, k_buf, v_buf = kv
        tiles_per_seq = seq_len // tm
        nkv = width // tn
        kv_tiles = (k_col0 // tn, v_col0 // tn, nkv)
        kv_spec = lambda c0: pl.BlockSpec(
            (None, None, tm, tn),
            lambda i, j: (i // tiles_per_seq, layer, i % tiles_per_seq, jnp.clip(j - c0, 0, nkv - 1)))
        out_specs += [kv_spec(kv_tiles[0]), kv_spec(kv_tiles[1])]
        out_shape += [jax.ShapeDtypeStruct((n // seq_len, n_layers, seq_len, width), F32)] * 2
        if k_buf is not None:
            n_alias = 2
            aliases = {len(args): 1, len(args) + 1: 2}
            in_specs += [pl.BlockSpec(memory_space=pl.ANY)] * 2
            args += [k_buf, v_buf]
    if emit_h:
        out_specs.append(pl.BlockSpec((tm, d), lambda i, j: (i, 0)))
        out_shape.append(jax.ShapeDtypeStruct((n, d), BF16))
        scratch = []
    else:
        scratch = [pltpu.VMEM((tm, d), BF16)]
    return pl.pallas_call(
        functools.partial(_normproj_kernel, n_alias=n_alias, kv_tiles=kv_tiles),
        grid=(n // tm, cols // tn),
        in_specs=in_specs,
        out_specs=out_specs,
        out_shape=out_shape,
        scratch_shapes=scratch,
        input_output_aliases=aliases,
        compiler_params=_cparams(("arbitrary", "arbitrary")),
        name="normproj_h" if emit_h else "normproj",
    )(*args)


def _tri_ones(tk):
    r = lax.broadcasted_iota(jnp.int32, (tk, tk + LANES), 0)
    c = lax.broadcasted_iota(jnp.int32, (tk, tk + LANES), 1)
    return jnp.where(c >= tk, 1.0, jnp.where(r > c, 1.0, 0.0)).astype(BF16)


def _sb_block(z, run, tri, valid):
    tk = z.shape[1]
    sp = jnp.maximum(z, 0.0) + jnp.log(1.0 + jnp.exp(-jnp.abs(z)))
    l1m = -sp if valid is None else jnp.where(valid, -sp, 0.0)
    sr = jnp.dot(l1m.astype(BF16), tri, preferred_element_type=F32)
    run_b = run if tk == LANES else jnp.concatenate([run] * (tk // LANES), axis=-1)
    a = jnp.exp((z - sp) + sr[:, :tk] + run_b)
    if valid is not None:
        a = jnp.where(valid, a, 0.0)
    return a, run + sr[:, tk:]


def _attn_kernel(bias_ref, q_ref, k_ref, v_ref, o_ref, q_scr, k_scr, v_scr, *, blk, hb):
    t = q_ref.shape[0]
    biases = [bias_ref[pl.program_id(1) * hb + i] for i in range(hb)]
    q_scr[...] = (q_ref[...] * SB_SCALE).astype(BF16)
    k_scr[...] = k_ref[...].astype(BF16)
    v_scr[...] = v_ref[...].astype(BF16)
    tri = _tri_ones(blk)
    row = lax.broadcasted_iota(jnp.int32, (blk, blk), 0)
    col = lax.broadcasted_iota(jnp.int32, (blk, blk), 1)
    diag_valid = col < row

    def q_block(qi, _):
        q0 = pl.multiple_of(qi * blk, blk)

        def k_block(k0, carry, valid):
            new = []
            for i, (acc, run) in enumerate(carry):
                ls = slice(i * HEAD_DIM, (i + 1) * HEAD_DIM)
                z = lax.dot_general(q_scr[pl.ds(q0, blk), ls], k_scr[pl.ds(k0, blk), ls], _NT,
                                    preferred_element_type=F32) + biases[i]
                a, run = _sb_block(z, run, tri, valid)
                acc = acc + jnp.dot(a.astype(BF16), v_scr[pl.ds(k0, blk), ls], preferred_element_type=F32)
                new.append((acc, run))
            return tuple(new)

        carry = ((jnp.zeros((blk, HEAD_DIM), F32), jnp.zeros((blk, LANES), F32)),) * hb
        carry = k_block(q0, carry, diag_valid)

        def body(s, carry):
            k0 = pl.multiple_of((qi - 1 - s) * blk, blk)
            return k_block(k0, carry, None)

        carry = lax.fori_loop(0, qi, body, carry)
        for i, (acc, _) in enumerate(carry):
            o_ref[pl.ds(q0, blk), i * HEAD_DIM:(i + 1) * HEAD_DIM] = acc
        return 0

    lax.fori_loop(0, t // blk, q_block, 0)


def _attention(u3, sb_bias, n_heads, q_col, k_col, v_col, blk=256, hb=4):
    b, t, _ = u3.shape
    assert n_heads % hb == 0 and q_col % hb == 0 and k_col % hb == 0 and v_col % hb == 0
    spec = lambda c0: pl.BlockSpec((None, t, hb * HEAD_DIM), lambda i, h: (i, 0, c0 // hb + h))
    return pl.pallas_call(
        functools.partial(_attn_kernel, blk=blk, hb=hb),
        grid=(b, n_heads // hb),
        in_specs=[pl.BlockSpec(memory_space=pltpu.SMEM), spec(q_col), spec(k_col), spec(v_col)],
        out_specs=pl.BlockSpec((None, t, hb * HEAD_DIM), lambda i, h: (i, 0, h)),
        out_shape=jax.ShapeDtypeStruct((b, t, n_heads * HEAD_DIM), F32),
        scratch_shapes=[pltpu.VMEM((t, hb * HEAD_DIM), BF16)] * 3,
        compiler_params=_cparams(("arbitrary", "arbitrary")),
        name="attn",
    )(sb_bias, u3, u3, u3)


def _dec_attn_kernel(pt_ref, q_ref, bias_ref, kn_ref, vn_ref, *refs, n_heads, n_sub):
    kc_refs, vc_refs = refs[:n_sub], refs[n_sub:2 * n_sub]
    o_ref, q_scr, acc_scr, run_scr = refs[2 * n_sub:]
    j = pl.program_id(1)
    rows = n_heads * DEC_Q_PAD
    head_rows = lambda h: slice(h * DEC_Q_PAD, (h + 1) * DEC_Q_PAD)
    head_cols = lambda h: slice(h * HEAD_DIM, (h + 1) * HEAD_DIM)

    @pl.when(j == 0)
    def _():
        for h in range(n_heads):
            q_scr[head_rows(h), :] = q_ref[:, head_cols(h)] * SB_SCALE
        acc_scr[...] = jnp.zeros_like(acc_scr)
        run_scr[...] = jnp.zeros_like(run_scr)

    tri = _tri_ones(PAGE_SIZE)
    of_head = lambda ref, h: ref[pl.ds(h, PAGE_SIZE, stride=n_heads), :].astype(BF16)

    def page(k_ref, v_ref, carry, valid):
        acc, run = carry
        z = jnp.concatenate(
            [lax.dot_general(q_scr[head_rows(h), :].astype(BF16), of_head(k_ref, h), _NT,
                             preferred_element_type=F32) for h in range(n_heads)], axis=0) + bias_ref[...]
        a, run = _sb_block(z, run, tri, valid)
        acc = acc + jnp.concatenate(
            [jnp.dot(a[head_rows(h), :].astype(BF16), of_head(v_ref, h), preferred_element_type=F32)
             for h in range(n_heads)], axis=0)
        return acc, run

    @pl.when(j == 0)
    def _():
        qi = jnp.bitwise_and(lax.broadcasted_iota(jnp.int32, (rows, PAGE_SIZE), 0), DEC_Q_PAD - 1)
        kc = lax.broadcasted_iota(jnp.int32, (rows, PAGE_SIZE), 1)
        acc, run = page(kn_ref, vn_ref, (acc_scr[...], run_scr[...]), kc < qi)
        acc_scr[...] = acc
        run_scr[...] = run

    @pl.when(j > 0)
    def _():
        carry = (acc_scr[...], run_scr[...])
        for s in range(n_sub):
            carry = page(kc_refs[s], vc_refs[s], carry, None)
        acc_scr[...] = carry[0]
        run_scr[...] = carry[1]

    @pl.when(j == pl.num_programs(1) - 1)
    def _():
        for h in range(n_heads):
            o_ref[:, head_cols(h)] = acc_scr[head_rows(h), :]


def _dec_attention(q, k_new, v_new, cache_k, cache_v, page_table, layer, sb_bias, n_sub=4):
    db, ts, width = q.shape
    n_heads = width // HEAD_DIM
    n_pages = page_table.shape[1]
    assert n_pages % n_sub == 0
    rows = n_heads * DEC_Q_PAD
    page_rows = PAGE_SIZE * n_heads
    q_pad = jnp.pad(q, ((0, 0), (0, DEC_Q_PAD - ts), (0, 0)))
    as_page = lambda x: jnp.pad(x.reshape(db, ts * n_heads, HEAD_DIM), ((0, 0), (0, page_rows - ts * n_heads), (0, 0)))
    bias = jnp.broadcast_to(jnp.repeat(sb_bias, DEC_Q_PAD)[:, None], (rows, PAGE_SIZE))
    per_seq = lambda r, w: pl.BlockSpec((None, r, w), lambda b, j, pt: (b, 0, 0))

    def cache_spec(s):
        page = lambda b, j, pt: pt[b, n_pages - 1 - (jnp.maximum(j - 1, 0) * n_sub + s)]
        return pl.BlockSpec((None, None, page_rows, HEAD_DIM), lambda b, j, pt: (page(b, j, pt), layer, 0, 0))

    out = pl.pallas_call(
        functools.partial(_dec_attn_kernel, n_heads=n_heads, n_sub=n_sub),
        grid_spec=pltpu.PrefetchScalarGridSpec(
            num_scalar_prefetch=1,
            grid=(db, n_pages // n_sub + 1),
            in_specs=[per_seq(DEC_Q_PAD, width), pl.BlockSpec((rows, PAGE_SIZE), lambda b, j, pt: (0, 0)),
                      per_seq(page_rows, HEAD_DIM), per_seq(page_rows, HEAD_DIM)]
            + [cache_spec(s) for s in range(n_sub)] * 2,
            out_specs=per_seq(DEC_Q_PAD, width),
            scratch_shapes=[pltpu.VMEM((rows, HEAD_DIM), F32), pltpu.VMEM((rows, HEAD_DIM), F32),
                            pltpu.VMEM((rows, LANES), F32)],
        ),
        out_shape=jax.ShapeDtypeStruct((db, DEC_Q_PAD, width), F32),
        compiler_params=_cparams(("arbitrary", "arbitrary")),
        name="dec_attn",
    )(page_table, q_pad, bias, as_page(k_new), as_page(v_new), *([cache_k] * n_sub), *([cache_v] * n_sub))
    return out[:, :ts]


def _mix_kernel(a_ref, ah_ref, val_ref, gate_ref, zh0_ref, zh1_ref, pw_ref, ps_ref, dw_ref, dwb_ref, lng_ref,
                lnb_ref, cpw_ref, gm_ref, ya_ref, yb_ref, pst_ref, cst_ref, fa_scr, fz_scr, y_scr,
                *, t_real, pos0, halo_is_state):
    t = pl.program_id(1)
    tt, pool_w = a_ref.shape
    conv_w = val_ref.shape[1]
    group_w = pool_w // len(POOL_WINDOWS)

    a = a_ref[...]
    ah = ah_ref[...]
    if not halo_is_state:
        ah = jnp.where(t == 0, 0.0, ah)
    fa_scr[0:POOL_HALO, :] = ah
    fa_scr[POOL_HALO:, :] = a

    @pl.when(t == pl.num_programs(1) - 1)
    def _():
        end = POOL_HALO + t_real
        pst_ref[...] = fa_scr[end - POOL_STATE:end, :]

    pos = pos0 + t * tt + lax.broadcasted_iota(jnp.int32, (tt, group_w), 0)
    outs = []
    for gi, win in enumerate(POOL_WINDOWS):
        cs = slice(gi * group_w, (gi + 1) * group_w)
        acc = fa_scr[POOL_HALO:POOL_HALO + tt, cs]
        for w in range(1, win):
            acc = acc + fa_scr[POOL_HALO - w:POOL_HALO - w + tt, cs]
        cnt = jnp.minimum(win, pos + 1).astype(F32)
        d = acc / cnt - a[:, cs]
        outs.append(jnp.dot(d.astype(BF16), pw_ref[gi], preferred_element_type=F32))
    ya = jnp.concatenate(outs, axis=-1) * ps_ref[...]
    ya_ref[...] = _rms(ya, gm_ref[:, 0:pool_w])

    if halo_is_state:
        zh = zh0_ref[...]
    else:
        zh = jnp.where(t == 0, 0.0, zh0_ref[...] * jax.nn.sigmoid(zh1_ref[...]))
    fz_scr[0:CONV_HALO, :] = zh
    fz_scr[CONV_HALO:, :] = val_ref[...] * jax.nn.sigmoid(gate_ref[...])

    @pl.when(t == pl.num_programs(1) - 1)
    def _():
        end = CONV_HALO + t_real
        cst_ref[...] = fz_scr[end - CONV_STATE:end, :]

    base = CONV_HALO - CONV_STATE
    for c in range(conv_w // LANES):
        cs = slice(c * LANES, (c + 1) * LANES)
        acc = jnp.zeros((tt, LANES), F32) + dwb_ref[:, cs]
        for lo in range(SUBLANES):
            n_hi = len(range(lo, CONV_K, SUBLANES))
            win = fz_scr[base + lo:base + lo + tt + SUBLANES * (n_hi - 1), cs]
            for hi in range(n_hi):
                j = SUBLANES * hi + lo
                acc = acc + dw_ref[j:j + 1, cs] * win[SUBLANES * hi:SUBLANES * hi + tt]
        y_scr[:, cs] = acc
    y = y_scr[...]
    mu = jnp.mean(y, axis=-1, keepdims=True)
    yc = y - mu
    var = jnp.mean(yc * yc, axis=-1, keepdims=True)
    yn = yc * lax.rsqrt(var + EPS) * lng_ref[...] + lnb_ref[...]
    s = yn * jax.nn.sigmoid(yn)
    yb = jnp.dot(s.astype(BF16), cpw_ref[...], preferred_element_type=F32)
    yb_ref[...] = _rms(yb, gm_ref[:, pool_w:pool_w + conv_w])


def _mix(u3, a_halo, z_halo, weights, gmix, *, tt, t_real, pos0, halo_is_state):
    b, t, _ = u3.shape
    pool_w16, pool_scale, conv_dw, conv_dw_b, ln_g, ln_b, conv_pw16 = weights
    w = pool_scale.shape[-1]
    nt = t // tt
    tile = lambda c: pl.BlockSpec((None, tt, w), lambda i, s: (i, s, c))
    if halo_is_state:
        ah_spec = pl.BlockSpec((None, POOL_HALO, w), lambda i, s: (i, 0, 0))
        zh_specs = [pl.BlockSpec((None, CONV_HALO, w), lambda i, s: (i, 0, 0))] * 2
        halos = (a_halo, z_halo, z_halo)
    else:
        prev = lambda rows, c: pl.BlockSpec(
            (None, rows, w), lambda i, s: (i, jnp.maximum(s * (tt // rows) - 1, 0), c))
        ah_spec = prev(POOL_HALO, 0)
        zh_specs = [prev(CONV_HALO, 1), prev(CONV_HALO, 2)]
        halos = (u3, u3, u3)
    full = lambda shape: pl.BlockSpec(shape, lambda i, s: (0,) * len(shape))
    row = lambda x: x.reshape(1, -1)
    return pl.pallas_call(
        functools.partial(_mix_kernel, t_real=t_real, pos0=pos0, halo_is_state=halo_is_state),
        grid=(b, nt),
        in_specs=[tile(0), ah_spec, tile(1), tile(2), *zh_specs,
                  full(pool_w16.shape), full((1, w)), full(conv_dw.shape), full((1, w)), full((1, w)),
                  full((1, w)), full(conv_pw16.shape), full((1, gmix.shape[-1]))],
        out_specs=[pl.BlockSpec((None, tt, w), lambda i, s: (i, s, 0))] * 2
        + [pl.BlockSpec((None, POOL_STATE, w), lambda i, s: (i, 0, 0)),
           pl.BlockSpec((None, CONV_STATE, w), lambda i, s: (i, 0, 0))],
        out_shape=[jax.ShapeDtypeStruct((b, t, w), F32)] * 2
        + [jax.ShapeDtypeStruct((b, POOL_STATE, w), F32), jax.ShapeDtypeStruct((b, CONV_STATE, w), F32)],
        scratch_shapes=[pltpu.VMEM((POOL_HALO + tt, w), F32), pltpu.VMEM((CONV_HALO + tt, w), F32),
                        pltpu.VMEM((tt, w), F32)],
        compiler_params=_cparams(("arbitrary", "arbitrary")),
        name="mix",
    )(u3, halos[0], u3, u3, halos[1], halos[2], pool_w16, row(pool_scale), conv_dw, row(conv_dw_b), row(ln_g),
      row(ln_b), conv_pw16, row(gmix))


def _outproj_kernel(ya_ref, yb_ref, yc_ref, gm_ref, w_ref, x_ref, g1_ref, o_ref, m_scr):
    @pl.when(pl.program_id(1) == 0)
    def _():
        wa, wb = ya_ref.shape[1], yb_ref.shape[1]
        m_scr[:, 0:wa] = ya_ref[...].astype(BF16)
        m_scr[:, wa:wa + wb] = yb_ref[...].astype(BF16)
        m_scr[:, wa + wb:] = _rms(yc_ref[...], gm_ref[:, wa + wb:]).astype(BF16)

    o_ref[...] = x_ref[...] + g1_ref[...] * jnp.dot(m_scr[...], w_ref[...], preferred_element_type=F32)


def _outproj(ya, yb, yc, gmix, w, x, mod3, k_gate, tm, tn):
    n, d = x.shape
    groups, r, _ = mod3.shape
    tiles_per_group = n // tm // groups
    rows = lambda width: pl.BlockSpec((tm, width), lambda i, j: (i, 0))
    return pl.pallas_call(
        _outproj_kernel,
        grid=(n // tm, d // tn),
        in_specs=[rows(ya.shape[1]), rows(yb.shape[1]), rows(yc.shape[1]),
                  pl.BlockSpec((1, d), lambda i, j: (0, 0)),
                  pl.BlockSpec((d, tn), lambda i, j: (0, j)),
                  pl.BlockSpec((tm, tn), lambda i, j: (i, j)),
                  pl.BlockSpec((None, r, tn), lambda i, j: (i // tiles_per_group, 0, k_gate * (d // tn) + j))],
        out_specs=pl.BlockSpec((tm, tn), lambda i, j: (i, j)),
        out_shape=jax.ShapeDtypeStruct((n, d), F32),
        scratch_shapes=[pltpu.VMEM((tm, d), BF16)],
        compiler_params=_cparams(("arbitrary", "arbitrary")),
        name="outproj",
    )(ya, yb, yc, gmix.reshape(1, d), w, x, mod3)


def _sort_network(n):
    pairs = []

    def merge(lo, hi, r):
        step = r * 2
        if step < hi - lo:
            merge(lo, hi, step)
            merge(lo + r, hi, step)
            pairs.extend((i, i + r) for i in range(lo + r, hi - r, step))
        else:
            pairs.append((lo, lo + r))

    def sort(lo, hi):
        if hi - lo >= 1:
            mid = lo + (hi - lo) // 2
            sort(lo, mid)
            sort(mid + 1, hi)
            merge(lo, hi, 1)

    sort(0, n - 1)
    return pairs


_SORT16 = _sort_network(PEER_TOPK)
_BITONIC16 = [(i, i + d) for d in (8, 4, 2, 1) for i in range(PEER_TOPK) if not i & d]


def _exchange(v, pairs):
    for i, j in pairs:
        v[i], v[j] = jnp.maximum(v[i], v[j]), jnp.minimum(v[i], v[j])
    return v


def _merge_sublanes(v):
    for shift in (4, 2, 1):
        other = [pltpu.roll(x, shift, 0) for x in v]
        v = [jnp.maximum(v[i], other[PEER_TOPK - 1 - i]) for i in range(PEER_TOPK)]
        v = _exchange(v, _BITONIC16)
    return v


def _top16_keys(s):
    v = [s[SUBLANES * i:SUBLANES * (i + 1), :] for i in range(N_KEYS // SUBLANES)]
    return _merge_sublanes(_exchange(v, _SORT16))


def _route_kernel(q_ref, sk_ref, s0_ref, s1_ref, tau_ref, beta_ref):
    sub = lax.broadcasted_iota(jnp.int32, (SUBLANES, LANES), 0)
    sk = [sk_ref[p].astype(BF16) for p in range(2)]
    tau_all = jnp.zeros((SUBLANES, LANES), F32)
    beta_all = jnp.zeros((SUBLANES, LANES), F32)
    for h in range(PEER_HEADS):
        top = []
        for p, s_ref in enumerate((s0_ref, s1_ref)):
            c0 = (2 * h + p) * PEER_HALF
            qc = q_ref[:, c0:c0 + PEER_HALF].astype(BF16)
            s = lax.dot_general(sk[p], qc, _NT, preferred_element_type=F32)
            s_ref[h] = s
            top.append(_top16_keys(s))
        a, b = top
        b_lo, b_hi = b[SUBLANES - 1], b[2 * SUBLANES - 1]
        for r in range(SUBLANES - 2, -1, -1):
            b_lo = jnp.where(sub == r, b[r], b_lo)
            b_hi = jnp.where(sub == r, b[SUBLANES + r], b_hi)
        cand = _exchange([x + b_lo for x in a], _SORT16)
        extra = a[0] + b_hi
        for i in range(PEER_TOPK):
            cand[i], extra = jnp.maximum(cand[i], extra), jnp.minimum(cand[i], extra)
        tv = _merge_sublanes(cand)
        zsum = jnp.ones((SUBLANES, LANES), F32)
        for r in range(1, PEER_TOPK):
            zsum = zsum + jnp.exp(tv[r] - tv[0])
        tau_all = jnp.where(sub == h, tv[PEER_TOPK - 1], tau_all)
        beta_all = jnp.where(sub == h, -(tv[0] + jnp.log(zsum)), beta_all)
    tau_ref[...] = tau_all
    beta_ref[...] = beta_all


def _route(q, subkeys):
    n, qc = q.shape
    score = pl.BlockSpec((PEER_HEADS, N_KEYS, LANES), lambda i: (0, 0, i))
    head_row = pl.BlockSpec((PEER_HEADS, LANES), lambda i: (0, i))
    return pl.pallas_call(
        _route_kernel,
        grid=(n // LANES,),
        in_specs=[pl.BlockSpec((LANES, qc), lambda i: (i, 0)),
                  pl.BlockSpec(subkeys.shape, lambda i: (0, 0, 0))],
        out_specs=[score, score, head_row, head_row],
        out_shape=[jax.ShapeDtypeStruct((PEER_HEADS, N_KEYS, n), F32)] * 2
        + [jax.ShapeDtypeStruct((PEER_HEADS, n), F32)] * 2,
        compiler_params=_cparams(("arbitrary",)),
        name="route",
    )(q, subkeys)


def _gelu_tanh(x):
    return 0.5 * x * (1.0 + jnp.tanh(0.7978845608028654 * (x + 0.044715 * (x * x * x))))


def _peer_kernel(h_ref, u_ref, v_ref, s0_ref, s1_ref, tau_ref, beta_ref, x_ref, g2_ref, o_ref, w_scr, *, sub):
    t = pl.program_id(1)
    te = u_ref.shape[0]
    tm = h_ref.shape[0]

    @pl.when(t == 0)
    def _():
        o_ref[...] = jnp.zeros_like(o_ref)

    h = h_ref[...]
    hts = [lax.dot_general(u_ref[s * sub:(s + 1) * sub, :], h, _NT, preferred_element_type=F32)
           for s in range(te // sub)]
    acc = o_ref[...]
    for s, ht in enumerate(hts):
        for ii in range(sub // N_KEYS):
            rows = slice(ii * N_KEYS, (ii + 1) * N_KEYS)
            for cc in range(tm // LANES):
                ls = slice(cc * LANES, (cc + 1) * LANES)
                gate = jnp.zeros((N_KEYS, LANES), F32)
                for hd in range(PEER_HEADS):
                    p = s0_ref[s * (sub // N_KEYS) + ii, hd:hd + 1, ls] + s1_ref[hd, :, ls]
                    gate = gate + jnp.where(p >= tau_ref[hd:hd + 1, ls], jnp.exp(p + beta_ref[hd:hd + 1, ls]), 0.0)
                w_scr[s, rows, ls] = _gelu_tanh(ht[rows, ls]) * gate
        w = w_scr[s].T.astype(BF16)
        acc = acc + jnp.dot(w, v_ref[s * sub:(s + 1) * sub, :], preferred_element_type=F32)
    o_ref[...] = acc

    @pl.when(t == pl.num_programs(1) - 1)
    def _():
        o_ref[...] = x_ref[...] + g2_ref[...] * o_ref[...]


def _peer(h, u16, v16, s0t, s1t, tau, beta, x, mod3, k_gate, tm, te, sub=256):
    n, d = x.shape
    n_exp = u16.shape[0]
    groups, r, _ = mod3.shape
    tiles_per_group = n // tm // groups
    once = dict(pipeline_mode=pl.Buffered(1))
    return pl.pallas_call(
        functools.partial(_peer_kernel, sub=sub),
        grid=(n // tm, n_exp // te),
        in_specs=[pl.BlockSpec((tm, d), lambda i, t: (i, 0), **once),
                  pl.BlockSpec((te, d), lambda i, t: (t, 0)),
                  pl.BlockSpec((te, d), lambda i, t: (t, 0)),
                  pl.BlockSpec((te // N_KEYS, PEER_HEADS, tm), lambda i, t: (t, 0, i)),
                  pl.BlockSpec((PEER_HEADS, N_KEYS, tm), lambda i, t: (0, 0, i), **once),
                  pl.BlockSpec((PEER_HEADS, tm), lambda i, t: (0, i)),
                  pl.BlockSpec((PEER_HEADS, tm), lambda i, t: (0, i)),
                  pl.BlockSpec((tm, d), lambda i, t: (i, 0), **once),
                  pl.BlockSpec((None, r, d), lambda i, t: (i // tiles_per_group, 0, k_gate))],
        out_specs=pl.BlockSpec((tm, d), lambda i, t: (i, 0)),
        out_shape=jax.ShapeDtypeStruct((n, d), F32),
        scratch_shapes=[pltpu.VMEM((te // sub, sub, tm), F32)],
        compiler_params=_cparams(("arbitrary", "arbitrary")),
        name="peer",
    )(h, u16, v16, jnp.swapaxes(s0t, 0, 1), s1t, tau, beta, x, mod3)


def _final_norm_kernel(x_ref, g_ref, o_ref):
    o_ref[...] = _rms(x_ref[...], g_ref[...])


def _final_norm(x, g, tm):
    n, d = x.shape
    return pl.pallas_call(
        _final_norm_kernel,
        grid=(n // tm,),
        in_specs=[pl.BlockSpec((tm, d), lambda i: (i, 0)), pl.BlockSpec((1, d), lambda i: (0, 0))],
        out_specs=pl.BlockSpec((tm, d), lambda i: (i, 0)),
        out_shape=jax.ShapeDtypeStruct((n, d), F32),
        compiler_params=_cparams(("arbitrary",)),
        name="final_norm",
    )(x, g.reshape(1, d))


def _trunk(x3, mod_rows, weights, final_g, *, tm, tn, tt, t_real_pad, pos0, peer_tm, peer_te, dec):
    b, t, d = x3.shape
    n = b * t
    x = x3.reshape(n, d)
    pool_w = weights["pool_scale"].shape[-1]
    conv_w = weights["conv_dw"].shape[-1]
    att_w = d - pool_w - conv_w
    n_heads = att_w // HEAD_DIM
    q_col = (pool_w + 2 * conv_w) // HEAD_DIM
    n_route = -(-n // LANES) * LANES
    new_k, new_v, new_pool, new_conv = [], [], [], []
    k_buf = v_buf = None
    for l in range(DEPTH):
        if dec is None:
            mod3 = mod_rows[l].reshape(b, 1, N_MOD * d)
        else:
            mod3 = jnp.repeat(mod_rows[l], t, axis=0).reshape(1, n, N_MOD * d)
        k_col0 = q_col * HEAD_DIM + att_w
        v_col0 = k_col0 + att_w
        if dec is None:
            u, k_buf, v_buf = _normproj(x, weights["norm1_g"][l], mod3, 1, 0, weights["w_in"][l], tm, tn,
                                        emit_h=False, kv=(l, k_col0, v_col0, att_w, t, DEPTH, k_buf, v_buf))
        else:
            u, = _normproj(x, weights["norm1_g"][l], mod3, 1, 0, weights["w_in"][l], tm, tn, emit_h=False)
            k = u[:, k_col0:v_col0].reshape(b, t, att_w)
            v = u[:, v_col0:].reshape(b, t, att_w)
            new_k.append(k)
            new_v.append(v)
        mix_w = (weights["pool_w"][l], weights["pool_scale"][l], weights["conv_dw"][l], weights["conv_dw_b"][l],
                 weights["conv_ln_g"][l], weights["conv_ln_b"][l], weights["conv_pw"][l])
        gmix = weights["mix_norm_g"][l]
        if dec is None:
            u3 = u.reshape(b, t, -1)
            ya, yb, pst, cst = _mix(u3, None, None, mix_w, gmix, tt=tt, t_real=tt, pos0=0, halo_is_state=False)
            yc = _attention(u3, weights["sb_bias"][l], n_heads, q_col, q_col + n_heads, q_col + 2 * n_heads)
        else:
            u3 = jnp.pad(u.reshape(b, t, -1), ((0, 0), (0, t_real_pad - t), (0, 0)))
            a_halo = jnp.pad(dec["state_pool"][l], ((0, 0), (POOL_HALO - POOL_STATE, 0), (0, 0)))
            z_halo = jnp.pad(dec["state_conv"][l], ((0, 0), (CONV_HALO - CONV_STATE, 0), (0, 0)))
            ya, yb, pst, cst = _mix(u3, a_halo, z_halo, mix_w, gmix, tt=t_real_pad, t_real=t, pos0=pos0,
                                    halo_is_state=True)
            ya, yb = ya[:, :t], yb[:, :t]
            q = u[:, q_col * HEAD_DIM:q_col * HEAD_DIM + att_w].reshape(b, t, att_w)
            yc = _dec_attention(q, k, v, dec["cache_k"], dec["cache_v"], dec["page_table"], l,
                                weights["sb_bias"][l])
        x = _outproj(ya.reshape(n, pool_w), yb.reshape(n, conv_w), yc.reshape(n, att_w), gmix,
                     weights["w_out"][l], x, mod3, 2, tm, tn)
        pq, h2 = _normproj(x, weights["norm2_g"][l], mod3, 4, 3, weights["peer_wq"][l], tm, tn, emit_h=True)
        if n_route != n:
            pad = ((0, n_route - n), (0, 0))
            pq, h2p, xp = jnp.pad(pq, pad), jnp.pad(h2, pad), jnp.pad(x, pad)
            mod3p = jnp.pad(mod3, ((0, 0), (0, n_route - n), (0, 0)))
        else:
            h2p, xp, mod3p = h2, x, mod3
        s0t, s1t, tau, beta = _route(pq, weights["peer_subkeys"][l])
        x = _peer(h2p, weights["peer_u"][l], weights["peer_v"][l], s0t, s1t, tau, beta, xp, mod3p, 5,
                  peer_tm, peer_te)[:n]
        new_pool.append(pst)
        new_conv.append(cst)
    y = _final_norm(x, final_g, min(tm, 512)).reshape(b, t, d)
    if dec is not None:
        k_buf, v_buf = jnp.stack(new_k, axis=1), jnp.stack(new_v, axis=1)
    heads = lambda buf: buf.reshape(b, DEPTH, t, n_heads, HEAD_DIM)
    return y, heads(k_buf), heads(v_buf), jnp.stack(new_pool, 0), jnp.stack(new_conv, 0)


def kernel(x_prompt, x_sample, c_prompt, c_sample, cache_k, cache_v, page_table, state_pool, state_conv, norm1_g, w_mod, b_mod, w_in, sb_bias, pool_w, pool_scale, conv_dw, conv_dw_b, conv_ln_g, conv_ln_b, conv_pw, mix_norm_g, w_out, norm2_g, peer_wq, peer_subkeys, peer_u, peer_v, final_g):
    bp, tp, d = x_prompt.shape
    bs, ts, _ = x_sample.shape
    weights = dict(norm1_g=norm1_g, w_in=w_in.astype(BF16), sb_bias=sb_bias, pool_w=pool_w.astype(BF16),
                   pool_scale=pool_scale, conv_dw=conv_dw, conv_dw_b=conv_dw_b, conv_ln_g=conv_ln_g,
                   conv_ln_b=conv_ln_b, conv_pw=conv_pw.astype(BF16), mix_norm_g=mix_norm_g,
                   w_out=w_out.astype(BF16), norm2_g=norm2_g, peer_wq=peer_wq.astype(BF16),
                   peer_subkeys=peer_subkeys, peer_u=peer_u.astype(BF16), peer_v=peer_v.astype(BF16))
    c_rows = bp + bs
    c_pad = -(-c_rows // SUBLANES) * SUBLANES
    c_all = jnp.pad(jnp.concatenate([c_prompt, c_sample], axis=0), ((0, c_pad - c_rows), (0, 0)))
    mod = _modulation(c_all, w_mod, b_mod)
    n_phys = cache_k.shape[0]
    dec = dict(cache_k=cache_k.reshape(n_phys, DEPTH, -1, HEAD_DIM),
               cache_v=cache_v.reshape(n_phys, DEPTH, -1, HEAD_DIM),
               page_table=page_table, state_pool=state_pool, state_conv=state_conv)
    y_p, k_p, v_p, pool_p, conv_p = _trunk(
        x_prompt, mod[:, :bp], weights, final_g, tm=512, tn=512, tt=256, t_real_pad=None, pos0=0,
        peer_tm=512, peer_te=512, dec=None)
    y_s, k_s, v_s, pool_s, conv_s = _trunk(
        x_sample, mod[:, bp:c_rows], weights, final_g, tm=bs * ts, tn=512, tt=None, t_real_pad=SUBLANES,
        pos0=page_table.shape[1] * PAGE_SIZE, peer_tm=LANES, peer_te=512, dec=dec)
    return (y_p, y_s, k_p, v_p, k_s, v_s, pool_p, pool_s, conv_p, conv_s)
```
